```python
import jax, jax.numpy as jnp
from jax import lax
import numpy as np

D_MODEL = 1024
BATCH = 16
SEQ = 2048
DEPTH = 1

CHUNK = 64
RMS_EPS = 1e-6
LN_EPS = 1e-5
D_FF = 2816
DN_HEADS = 8
DN_DK = 128
DN_DV = 128
DN_CONV = 4
DN_QK = DN_HEADS * DN_DK
DN_V = DN_HEADS * DN_DV
GM_GROUPS = 8
GM_WIDTH = 1024
GM_CH = GM_WIDTH // GM_GROUPS
GM_CHUNK = 128
D_IN = 2 * DN_QK + 2 * DN_V + 2 * DN_HEADS + 2 * GM_WIDTH + 2 * D_MODEL

kernel_name = "hybrid_deltanet_gmlp_macaron"


def rmsnorm(x, w):
    x32 = x.astype(jnp.float32)
    y = x32 * lax.rsqrt(jnp.mean(x32 * x32, axis=-1, keepdims=True) + RMS_EPS)
    return (y * w.astype(jnp.float32)).astype(x.dtype)


def layernorm(x, w, b):
    x32 = x.astype(jnp.float32)
    mu = jnp.mean(x32, axis=-1, keepdims=True)
    xc = x32 - mu
    y = xc * lax.rsqrt(jnp.mean(xc * xc, axis=-1, keepdims=True) + LN_EPS)
    return (y * w.astype(jnp.float32) + b.astype(jnp.float32)).astype(x.dtype)


def l2norm(x):
    return x * lax.rsqrt(jnp.sum(x * x, axis=-1, keepdims=True) + 1e-6)


def swiglu_ffn(x, w_gate, w_up, w_down):
    return (jax.nn.silu(x @ w_gate) * (x @ w_up)) @ w_down


def causal_depthwise_conv(x, w):
    k_len, c = w.shape
    return lax.conv_general_dilated(
        x, w[:, None, :].astype(x.dtype), window_strides=(1,),
        padding=[(k_len - 1, 0)], dimension_numbers=("NWC", "WIO", "NWC"),
        feature_group_count=c)


def gated_delta_rule_chunked(q, k, v, beta, log_alpha):
    f32 = jnp.float32
    b, s, h, dk = q.shape
    dv = v.shape[-1]
    n = s // CHUNK

    def chunks(t):
        return jnp.moveaxis(t.astype(f32).reshape((b, n, CHUNK) + t.shape[2:]), 2, 3)

    q = chunks(q) * (dk ** -0.5)
    k = chunks(k)
    v = chunks(v)
    beta = chunks(beta)
    g = jnp.cumsum(chunks(log_alpha), axis=-1)

    idx = jnp.arange(CHUNK)
    causal = idx[:, None] >= idx[None, :]
    strict = idx[:, None] > idx[None, :]
    gdiff = g[..., :, None] - g[..., None, :]
    decay = jnp.where(causal, jnp.exp(jnp.where(causal, gdiff, 0.0)), 0.0)

    kk = jnp.einsum("bnhid,bnhjd->bnhij", k, k)
    a_mat = jnp.where(strict, beta[..., :, None] * kk * decay, 0.0) + jnp.eye(CHUNK, dtype=f32)
    rhs = jnp.concatenate([v * beta[..., None], k * (beta * jnp.exp(g))[..., None]], axis=-1)
    sol = lax.linalg.triangular_solve(a_mat, rhs, left_side=True, lower=True, unit_diagonal=True)
    u, w = sol[..., :dv], sol[..., dv:]

    qk = jnp.einsum("bnhid,bnhjd->bnhij", q, k) * decay
    q_dec = q * jnp.exp(g)[..., None]
    k_dec = k * jnp.exp(g[..., -1:] - g)[..., None]
    g_last = jnp.exp(g[..., -1])

    def step(state, xs):
        qd, kd, u_c, w_c, qk_c, gl = xs
        v_new = u_c - jnp.einsum("bhcd,bhde->bhce", w_c, state)
        o = (jnp.einsum("bhcd,bhde->bhce", qd, state)
             + jnp.einsum("bhij,bhje->bhie", qk_c, v_new))
        state = state * gl[..., None, None] + jnp.einsum("bhcd,bhce->bhde", kd, v_new)
        return state, o

    xs = tuple(jnp.moveaxis(t, 1, 0) for t in (q_dec, k_dec, u, w, qk, g_last))
    state0 = jnp.zeros((b, h, dk, dv), f32)
    _, o = lax.scan(step, state0, xs)
    o = jnp.moveaxis(jnp.moveaxis(o, 0, 1), 3, 2)
    return o.reshape(b, s, h, dv)


def spatial_gating(u, v, norm_w, norm_b, w_s, b_s):
    b, s, _ = v.shape
    n_grp = s // GM_CHUNK
    v = layernorm(v, norm_w, norm_b)
    pos = jnp.arange(GM_CHUNK)
    mask = (pos[None, :] // CHUNK) <= (pos[:, None] // CHUNK)
    w_m = jnp.where(mask[None], w_s, 0.0).astype(v.dtype)
    vg = v.reshape(b, n_grp, GM_CHUNK, GM_GROUPS, GM_CH)
    mixed = (jnp.einsum("hij,bgjhc->bgihc", w_m, vg)
             + jnp.swapaxes(b_s, 0, 1).astype(v.dtype)[None, None, :, :, None])
    return u * mixed.reshape(b, s, GM_WIDTH)


def hybrid_mixer(h, norm_w, w_in, conv_w, a_log, dt_bias, dn_norm_w, dn_w_o,
                 sgu_norm_w, sgu_norm_b, sgu_w_s, sgu_b, gm_w_o, w_out):
    b, s, _ = h.shape
    f32 = jnp.float32
    n = rmsnorm(h, norm_w)
    proj = n @ w_in
    qkv_w = 2 * DN_QK + DN_V
    sizes = (qkv_w, DN_V, DN_HEADS, DN_HEADS, GM_WIDTH, GM_WIDTH, D_MODEL)
    points = [int(p) for p in np.cumsum(sizes)]
    qkv, g_out, b_raw, a_raw, gm_u, gm_v, gate_a, gate_b = jnp.split(proj, points, axis=-1)

    qkv = jax.nn.silu(causal_depthwise_conv(qkv, conv_w))
    q, k, v = jnp.split(qkv, [DN_QK, 2 * DN_QK], axis=-1)
    q = l2norm(q.astype(f32).reshape(b, s, DN_HEADS, DN_DK))
    k = l2norm(k.astype(f32).reshape(b, s, DN_HEADS, DN_DK))
    v = v.reshape(b, s, DN_HEADS, DN_DV)
    beta = jax.nn.sigmoid(b_raw.astype(f32))
    log_alpha = -jnp.exp(a_log.astype(f32)) * jax.nn.softplus(a_raw.astype(f32) + dt_bias.astype(f32))
    o = gated_delta_rule_chunked(q, k, v, beta, log_alpha)
    o = rmsnorm(o, dn_norm_w) * jax.nn.silu(g_out.astype(f32).reshape(b, s, DN_HEADS, DN_DV))
    y_a = o.reshape(b, s, DN_V).astype(h.dtype) @ dn_w_o

    sg = spatial_gating(jax.nn.gelu(gm_u, approximate=False), jax.nn.gelu(gm_v, approximate=False),
                        sgu_norm_w, sgu_norm_b, sgu_w_s, sgu_b)
    y_b = sg @ gm_w_o

    merged = jax.nn.sigmoid(gate_a) * y_a + jax.nn.sigmoid(gate_b) * y_b
    return merged @ w_out


def setup_inputs(seed: int = 0) -> dict:
    key = jax.random.key(seed)
    ks = iter(jax.random.split(key, 32))
    f32 = jnp.float32

    def dense(shape, fan_in):
        return jax.random.normal(next(ks), shape, f32) * (fan_in ** -0.5)

    def gain(shape):
        return 1.0 + 0.05 * jax.random.normal(next(ks), shape, f32)

    L = DEPTH
    x = jax.random.normal(next(ks), (BATCH, SEQ, D_MODEL), f32)
    ffn1_norm_w = gain((L, D_MODEL))
    ffn1_w_gate = dense((L, D_MODEL, D_FF), D_MODEL)
    ffn1_w_up = dense((L, D_MODEL, D_FF), D_MODEL)
    ffn1_w_down = dense((L, D_FF, D_MODEL), D_FF)
    mix_norm_w = gain((L, D_MODEL))
    w_in = dense((L, D_MODEL, D_IN), D_MODEL)
    dn_conv_w = dense((L, DN_CONV, 2 * DN_QK + DN_V), DN_CONV)
    dn_a_log = jnp.log(jax.random.uniform(next(ks), (L, DN_HEADS), f32, 1.0, 16.0))
    dt = jnp.exp(jax.random.uniform(next(ks), (L, DN_HEADS), f32, np.log(1e-3), np.log(1e-1)))
    dn_dt_bias = dt + jnp.log(-jnp.expm1(-dt))
    dn_out_norm_w = gain((L, DN_DV))
    dn_w_o = dense((L, DN_V, D_MODEL), DN_V)
    sgu_norm_w = gain((L, GM_WIDTH))
    sgu_norm_b = 0.02 * jax.random.normal(next(ks), (L, GM_WIDTH), f32)
    sgu_w_s = 0.5 * dense((L, GM_GROUPS, GM_CHUNK, GM_CHUNK), GM_CHUNK)
    sgu_b = gain((L, GM_GROUPS, GM_CHUNK))
    gmlp_w_o = dense((L, GM_WIDTH, D_MODEL), GM_WIDTH)
    w_out = dense((L, D_MODEL, D_MODEL), D_MODEL)
    ffn2_norm_w = gain((L, D_MODEL))
    ffn2_w_gate = dense((L, D_MODEL, D_FF), D_MODEL)
    ffn2_w_up = dense((L, D_MODEL, D_FF), D_MODEL)
    ffn2_w_down = dense((L, D_FF, D_MODEL), D_FF)
    final_norm_w = gain((D_MODEL,))
    return {"x": x, "ffn1_norm_w": ffn1_norm_w, "ffn1_w_gate": ffn1_w_gate, "ffn1_w_up": ffn1_w_up,
            "ffn1_w_down": ffn1_w_down, "mix_norm_w": mix_norm_w, "w_in": w_in, "dn_conv_w": dn_conv_w,
            "dn_a_log": dn_a_log, "dn_dt_bias": dn_dt_bias, "dn_out_norm_w": dn_out_norm_w,
            "dn_w_o": dn_w_o, "sgu_norm_w": sgu_norm_w, "sgu_norm_b": sgu_norm_b, "sgu_w_s": sgu_w_s,
            "sgu_b": sgu_b, "gmlp_w_o": gmlp_w_o, "w_out": w_out, "ffn2_norm_w": ffn2_norm_w,
            "ffn2_w_gate": ffn2_w_gate, "ffn2_w_up": ffn2_w_up, "ffn2_w_down": ffn2_w_down,
            "final_norm_w": final_norm_w}


def reference(x, ffn1_norm_w, ffn1_w_gate, ffn1_w_up, ffn1_w_down, mix_norm_w, w_in, dn_conv_w,
              dn_a_log, dn_dt_bias, dn_out_norm_w, dn_w_o, sgu_norm_w, sgu_norm_b, sgu_w_s, sgu_b,
              gmlp_w_o, w_out, ffn2_norm_w, ffn2_w_gate, ffn2_w_up, ffn2_w_down, final_norm_w):
    h = x
    for l in range(DEPTH):
        h = h + 0.5 * swiglu_ffn(rmsnorm(h, ffn1_norm_w[l]), ffn1_w_gate[l], ffn1_w_up[l], ffn1_w_down[l])
        h = h + hybrid_mixer(h, mix_norm_w[l], w_in[l], dn_conv_w[l], dn_a_log[l], dn_dt_bias[l],
                             dn_out_norm_w[l], dn_w_o[l], sgu_norm_w[l], sgu_norm_b[l], sgu_w_s[l],
                             sgu_b[l], gmlp_w_o[l], w_out[l])
        h = h + 0.5 * swiglu_ffn(rmsnorm(h, ffn2_norm_w[l]), ffn2_w_gate[l], ffn2_w_up[l], ffn2_w_down[l])
    return rmsnorm(h, final_norm_w)
```

```python
import functools

import jax
import jax.numpy as jnp
from jax import lax
from jax.experimental import pallas as pl
from jax.experimental.pallas import tpu as pltpu

F32 = jnp.float32
BF16 = jnp.bfloat16

RMS_EPS = 1e-6
LN_EPS = 1e-5
L2_EPS = 1e-6
CHUNK = 64
GM_CHUNK = 128
HEAD = 128
CONV_K = 4
LANE = 128
CARRY = 8

FFN_TOKENS = 512
FFN_COLS = 256
MIX_TOKENS = 256
DELTA_TOKENS = 512
VMEM_LIMIT = 56 * 1024 * 1024


def _resident(shape):
    nd = len(shape)
    return pl.BlockSpec(shape, lambda *_: (0,) * nd, pipeline_mode=pl.Buffered(1))


def _dot(a, b):
    return jnp.dot(a, b, preferred_element_type=F32)


def _rms(x, w):
    return x * lax.rsqrt(jnp.mean(x * x, axis=-1, keepdims=True) + RMS_EPS) * w


def _silu(x):
    return x * jax.nn.sigmoid(x)


def _gelu(x):
    return 0.5 * x * (1.0 + lax.erf(x * 0.7071067811865476))


def _ffn_kernel(x_ref, nw_ref, wg_ref, wu_ref, wd_ref, fw_ref, o_ref, n_scr, a_scr, *, final_norm):
    n_scr[...] = _rms(x_ref[...], nw_ref[...]).astype(BF16)
    d_ff = wg_ref.shape[1]
    for j in range(d_ff // FFN_COLS):
        sl = slice(j * FFN_COLS, (j + 1) * FFN_COLS)
        n = n_scr[...]
        g = _dot(n, wg_ref[:, sl])
        u = _dot(n, wu_ref[:, sl])
        a_scr[:, sl] = (_silu(g) * u).astype(BF16)
    h = x_ref[...] + 0.5 * _dot(a_scr[...], wd_ref[...])
    if final_norm:
        h = _rms(h, fw_ref[...])
    o_ref[...] = h


def _ffn(x2d, norm_w, wg, wu, wd, final_w, final_norm):
    t, d = x2d.shape
    d_ff = wg.shape[1]
    tm = FFN_TOKENS
    row = pl.BlockSpec((tm, d), lambda i: (i, 0))
    return pl.pallas_call(
        functools.partial(_ffn_kernel, final_norm=final_norm),
        grid=(t // tm,),
        in_specs=[row, _resident((1, d)), _resident((d, d_ff)), _resident((d, d_ff)),
                  _resident((d_ff, d)), _resident((1, d))],
        out_specs=row,
        out_shape=jax.ShapeDtypeStruct((t, d), F32),
        scratch_shapes=[pltpu.VMEM((tm, d), BF16), pltpu.VMEM((tm, d_ff), BF16)],
        compiler_params=pltpu.CompilerParams(dimension_semantics=("arbitrary",),
                                             vmem_limit_bytes=VMEM_LIMIT),
        name="ffn_final" if final_norm else "ffn",
    )(x2d, norm_w, wg, wu, wd, final_w)


def _mix_in_kernel(h_ref, nw_ref, wqkv_ref, wg_ref, wba_ref, wu_ref, wv_ref, wga_ref, wgb_ref,
                   cw_ref, lnw_ref, lnb_ref,
                   q_ref, k_ref, v_ref, gs_ref, ba_ref, gu_ref, vln_ref, sga_ref, sgb_ref,
                   n_scr, conv_scr):
    ts = h_ref.shape[0]
    d = h_ref.shape[1]

    @pl.when(pl.program_id(1) == 0)
    def _():
        conv_scr[0:CARRY, :] = jnp.zeros((CARRY, conv_scr.shape[1]), F32)

    n_scr[...] = _rms(h_ref[...], nw_ref[...]).astype(BF16)

    n_heads = d // HEAD
    for j in range(3 * n_heads):
        sl = slice(j * HEAD, (j + 1) * HEAD)
        conv_scr[CARRY:CARRY + ts, sl] = _dot(n_scr[...], wqkv_ref[:, sl])
        acc = cw_ref[CONV_K - 1:CONV_K, sl] * conv_scr[CARRY:CARRY + ts, sl]
        for kk in range(CONV_K - 1):
            off = CARRY - (CONV_K - 1) + kk
            acc = acc + cw_ref[kk:kk + 1, sl] * conv_scr[off:off + ts, sl]
        y = _silu(acc)
        if j < 2 * n_heads:
            y = y * lax.rsqrt(jnp.sum(y * y, axis=-1, keepdims=True) + L2_EPS)
        dst = (q_ref, k_ref, v_ref)[j // n_heads]
        hs = slice((j % n_heads) * HEAD, (j % n_heads + 1) * HEAD)
        dst[:, hs] = y.astype(BF16)
    conv_scr[0:CARRY, :] = conv_scr[ts:ts + CARRY, :]

    n = n_scr[...]
    gs_ref[...] = _silu(_dot(n, wg_ref[...])).astype(BF16)
    ba_ref[...] = _dot(n, wba_ref[...])
    gu_ref[...] = _gelu(_dot(n, wu_ref[...])).astype(BF16)
    gv = _gelu(_dot(n, wv_ref[...]))
    mu = jnp.mean(gv, axis=-1, keepdims=True)
    xc = gv - mu
    ln = xc * lax.rsqrt(jnp.mean(xc * xc, axis=-1, keepdims=True) + LN_EPS)
    vln_ref[...] = (ln * lnw_ref[...] + lnb_ref[...]).astype(BF16)
    sga_ref[...] = jax.nn.sigmoid(_dot(n, wga_ref[...])).astype(BF16)
    sgb_ref[...] = jax.nn.sigmoid(_dot(n, wgb_ref[...])).astype(BF16)


def _mix_in(h, norm_w, wqkv, wg, wba, wu, wv, wga, wgb, conv_w, ln_w, ln_b):
    b, s, d = h.shape
    ts = MIX_TOKENS
    tile = lambda w: pl.BlockSpec((None, ts, w), lambda i, j: (i, j, 0))
    bf = lambda w: jax.ShapeDtypeStruct((b, s, w), BF16)
    return pl.pallas_call(
        _mix_in_kernel,
        grid=(b, s // ts),
        in_specs=[tile(d), _resident((1, d)), _resident(wqkv.shape), _resident(wg.shape),
                  _resident(wba.shape), _resident(wu.shape), _resident(wv.shape),
                  _resident(wga.shape), _resident(wgb.shape), _resident(conv_w.shape),
                  _resident((1, d)), _resident((1, d))],
        out_specs=[tile(d), tile(d), tile(d), tile(d), tile(LANE), tile(d), tile(d), tile(d), tile(d)],
        out_shape=[bf(d), bf(d), bf(d), bf(d), jax.ShapeDtypeStruct((b, s, LANE), F32),
                   bf(d), bf(d), bf(d), bf(d)],
        scratch_shapes=[pltpu.VMEM((ts, d), BF16), pltpu.VMEM((ts + CARRY, 3 * d), F32)],
        compiler_params=pltpu.CompilerParams(dimension_semantics=("arbitrary", "arbitrary"),
                                             vmem_limit_bytes=VMEM_LIMIT),
        name="mix_in",
    )(h, norm_w, wqkv, wg, wba, wu, wv, wga, wgb, conv_w, ln_w, ln_b)


def _delta_kernel(q_ref, k_ref, v_ref, ba_ref, gs_ref, hp_ref, onw_ref, o_ref, s_scr):
    ts, d = q_ref.shape
    n_heads = d // HEAD
    c = CHUNK

    @pl.when(pl.program_id(1) == 0)
    def _():
        s_scr[...] = jnp.zeros(s_scr.shape, F32)

    row = lax.broadcasted_iota(jnp.int32, (c, c), 0)
    col = lax.broadcasted_iota(jnp.int32, (c, c), 1)
    causal = row >= col
    strict = row > col
    tri = causal.astype(F32)
    eye = (row == col).astype(F32)
    scale = HEAD ** -0.5

    def chunk_body(ci, carry):
        r0 = pl.multiple_of(ci * c, c)
        rows = pl.ds(r0, c)
        ba = ba_ref[rows, :]
        beta_all = jax.nn.sigmoid(ba)
        z = ba + hp_ref[1:2, :]
        softplus = jnp.maximum(z, 0.0) + jnp.log1p(jnp.exp(-jnp.abs(z)))
        la = -jnp.exp(hp_ref[0:1, :]) * softplus
        g_all = jnp.dot(tri, la, preferred_element_type=F32, precision=lax.Precision.HIGHEST)
        g_t = g_all.T
        eg_all = jnp.exp(g_all)
        g_last_all = g_all[c - 1:c, :]
        ekd_all = jnp.exp(g_last_all - g_all)
        egl_all = jnp.exp(g_last_all)
        for h in range(n_heads):
            hs = slice(h * HEAD, (h + 1) * HEAD)
            ga = n_heads + h
            qh = q_ref[rows, hs]
            kh = k_ref[rows, hs]
            vh = v_ref[rows, hs].astype(F32)
            khf = kh.astype(F32)
            beta = beta_all[:, h:h + 1]
            gcol = g_all[:, ga:ga + 1]
            grow = g_t[ga:ga + 1, :]
            decay = jnp.where(causal, jnp.exp(jnp.where(causal, gcol - grow, 0.0)), 0.0)
            qk_kk = lax.dot_general(jnp.concatenate([qh, kh], axis=0), kh,
                                    (((1,), (1,)), ((), ())), preferred_element_type=F32)
            qkm = qk_kk[:c] * (decay * scale)
            lmat = jnp.where(strict, beta * qk_kk[c:] * decay, 0.0)
            m = (-lmat).astype(BF16)
            p = eye - lmat
            sq = c
            while sq > 2:
                m32 = _dot(m, m)
                m = m32.astype(BF16)
                p = p + _dot(m, p.astype(BF16))
                sq //= 2
            eg = eg_all[:, ga:ga + 1]
            rhs = jnp.concatenate([vh * beta, khf * (beta * eg)], axis=1).astype(BF16)
            sol = _dot(p.astype(BF16), rhs)
            u = sol[:, :HEAD]
            w = sol[:, HEAD:]
            q_dec = qh.astype(F32) * (eg * scale)
            k_dec = khf * ekd_all[:, ga:ga + 1]
            gl = egl_all[:, ga:ga + 1]

            state = s_scr[h]
            ws = _dot(jnp.concatenate([w, q_dec], axis=0).astype(BF16), state.astype(BF16))
            v_new = (u - ws[:c]).astype(BF16)
            o = ws[c:] + _dot(qkm.astype(BF16), v_new)
            ds = lax.dot_general(k_dec.astype(BF16), v_new, (((0,), (0,)), ((), ())),
                                 preferred_element_type=F32)
            s_scr[h] = state * gl + ds

            on = o * lax.rsqrt(jnp.mean(o * o, axis=-1, keepdims=True) + RMS_EPS) * onw_ref[...]
            o_ref[rows, hs] = (on * gs_ref[rows, hs].astype(F32)).astype(BF16)
        return carry

    lax.fori_loop(0, ts // c, chunk_body, 0)


def _delta(q, k, v, ba, gs, hp, onw):
    b, s, d = q.shape
    ts = DELTA_TOKENS
    tile = lambda w: pl.BlockSpec((None, ts, w), lambda i, j: (i, j, 0))
    return pl.pallas_call(
        _delta_kernel,
        grid=(b, s // ts),
        in_specs=[tile(d), tile(d), tile(d), tile(LANE), tile(d), _resident(hp.shape),
                  _resident(onw.shape)],
        out_specs=tile(d),
        out_shape=jax.ShapeDtypeStruct((b, s, d), BF16),
        scratch_shapes=[pltpu.VMEM((d // HEAD, HEAD, HEAD), F32)],
        compiler_params=pltpu.CompilerParams(dimension_semantics=("arbitrary", "arbitrary"),
                                             vmem_limit_bytes=VMEM_LIMIT),
        name="delta",
    )(q, k, v, ba, gs, hp, onw)


def _mix_out_kernel(h_ref, og_ref, gu_ref, vln_ref, sga_ref, sgb_ref, ws_ref, bs_ref,
                    wdo_ref, wgo_ref, wout_ref, o_ref, sg_scr):
    ts, d = h_ref.shape
    n_groups = d // HEAD
    gc = GM_CHUNK
    pos_r = lax.broadcasted_iota(jnp.int32, (gc, gc), 0) // CHUNK
    pos_c = lax.broadcasted_iota(jnp.int32, (gc, gc), 1) // CHUNK
    mask = pos_c <= pos_r
    for h in range(n_groups):
        hs = slice(h * HEAD, (h + 1) * HEAD)
        w_m = jnp.where(mask, ws_ref[h], 0.0).astype(BF16)
        bias = bs_ref[:, h:h + 1]
        for g in range(ts // gc):
            rows = slice(g * gc, (g + 1) * gc)
            mixed = _dot(w_m, vln_ref[rows, hs]) + bias
            sg_scr[rows, hs] = (gu_ref[rows, hs].astype(F32) * mixed).astype(BF16)
    y_a = _dot(og_ref[...], wdo_ref[...])
    y_b = _dot(sg_scr[...], wgo_ref[...])
    merged = sga_ref[...].astype(F32) * y_a + sgb_ref[...].astype(F32) * y_b
    o_ref[...] = h_ref[...] + _dot(merged.astype(BF16), wout_ref[...])


def _mix_out(h, og, gu, vln, sga, sgb, w_s, b_s_t, wdo, wgo, wout):
    b, s, d = h.shape
    ts = MIX_TOKENS
    tile = pl.BlockSpec((None, ts, d), lambda i, j: (i, j, 0))
    return pl.pallas_call(
        _mix_out_kernel,
        grid=(b, s // ts),
        in_specs=[tile] * 6 + [_resident(w_s.shape), _resident(b_s_t.shape), _resident(wdo.shape),
                               _resident(wgo.shape), _resident(wout.shape)],
        out_specs=tile,
        out_shape=jax.ShapeDtypeStruct((b, s, d), F32),
        scratch_shapes=[pltpu.VMEM((ts, d), BF16)],
        compiler_params=pltpu.CompilerParams(dimension_semantics=("arbitrary", "arbitrary"),
                                             vmem_limit_bytes=VMEM_LIMIT),
        name="mix_out",
    )(h, og, gu, vln, sga, sgb, w_s, b_s_t, wdo, wgo, wout)


def _layer(h, p):
    b, s, d = h.shape
    n_heads = d // HEAD
    row = lambda a: a.reshape(1, -1).astype(F32)
    bf = lambda a: a.astype(BF16)
    unused_w = jnp.ones((1, d), F32)

    h1 = _ffn(h.reshape(b * s, d), row(p["ffn1_norm_w"]), bf(p["ffn1_w_gate"]), bf(p["ffn1_w_up"]),
              bf(p["ffn1_w_down"]), unused_w, False).reshape(b, s, d)

    w_in = p["w_in"]
    bounds = [0, 3 * d, 4 * d, 4 * d + 2 * n_heads]
    bounds += [bounds[-1] + d * i for i in range(1, 5)]
    wqkv, wg, wba, wu, wv, wga, wgb = (w_in[:, lo:hi] for lo, hi in zip(bounds[:-1], bounds[1:]))
    wba = jnp.pad(wba, ((0, 0), (0, LANE - 2 * n_heads)))
    q, k, v, gs, ba, gu, vln, sga, sgb = _mix_in(
        h1, row(p["mix_norm_w"]), bf(wqkv), bf(wg), bf(wba), bf(wu), bf(wv), bf(wga), bf(wgb),
        p["dn_conv_w"].astype(F32), row(p["sgu_norm_w"]), row(p["sgu_norm_b"]))

    hp = jnp.zeros((2, LANE), F32)
    hp = hp.at[0, n_heads:2 * n_heads].set(p["dn_a_log"].astype(F32))
    hp = hp.at[1, n_heads:2 * n_heads].set(p["dn_dt_bias"].astype(F32))
    og = _delta(q, k, v, ba, gs, hp, row(p["dn_out_norm_w"]))

    h2 = _mix_out(h1, og, gu, vln, sga, sgb, p["sgu_w_s"].astype(F32),
                  jnp.swapaxes(p["sgu_b"], 0, 1).astype(F32),
                  bf(p["dn_w_o"]), bf(p["gmlp_w_o"]), bf(p["w_out"]))
    return h2


def kernel(x, ffn1_norm_w, ffn1_w_gate, ffn1_w_up, ffn1_w_down, mix_norm_w, w_in, dn_conv_w, dn_a_log, dn_dt_bias, dn_out_norm_w, dn_w_o, sgu_norm_w, sgu_norm_b, sgu_w_s, sgu_b, gmlp_w_o, w_out, ffn2_norm_w, ffn2_w_gate, ffn2_w_up, ffn2_w_down, final_norm_w):
    depth = ffn1_norm_w.shape[0]
    assert depth >= 1
    b, s, d = x.shape
    h = x
    for l in range(depth):
        p = dict(ffn1_norm_w=ffn1_norm_w[l], ffn1_w_gate=ffn1_w_gate[l], ffn1_w_up=ffn1_w_up[l],
                 ffn1_w_down=ffn1_w_down[l], mix_norm_w=mix_norm_w[l], w_in=w_in[l],
                 dn_conv_w=dn_conv_w[l], dn_a_log=dn_a_log[l], dn_dt_bias=dn_dt_bias[l],
                 dn_out_norm_w=dn_out_norm_w[l], dn_w_o=dn_w_o[l], sgu_norm_w=sgu_norm_w[l],
                 sgu_norm_b=sgu_norm_b[l], sgu_w_s=sgu_w_s[l], sgu_b=sgu_b[l], gmlp_w_o=gmlp_w_o[l],
                 w_out=w_out[l])
        h2 = _layer(h, p)
        last = l == depth - 1
        bf = lambda a: a.astype(BF16)
        h = _ffn(h2.reshape(b * s, d), ffn2_norm_w[l].reshape(1, d), bf(ffn2_w_gate[l]),
                 bf(ffn2_w_up[l]), bf(ffn2_w_down[l]), final_norm_w.reshape(1, d),
                 last).reshape(b, s, d)
    return h
```

```python
import functools

import jax
import jax.numpy as jnp
from jax import lax
from jax.experimental import pallas as pl
from jax.experimental.pallas import tpu as pltpu

F32 = jnp.float32
BF16 = jnp.bfloat16

RMS_EPS = 1e-6
LN_EPS = 1e-5
L2_EPS = 1e-6
CHUNK = 64
GM_CHUNK = 128
HEAD = 128
CONV_K = 4
LANE = 128
CARRY = 8

FFN_TOKENS = 512
FFN_COLS = 256
MIX_TOKENS = 256
DELTA_TOKENS = 512
VMEM_LIMIT = 56 * 1024 * 1024


def _resident(shape):
    nd = len(shape)
    return pl.BlockSpec(shape, lambda *_: (0,) * nd, pipeline_mode=pl.Buffered(1))


def _dot(a, b):
    return jnp.dot(a, b, preferred_element_type=F32)


def _rms(x, w):
    return x * lax.rsqrt(jnp.mean(x * x, axis=-1, keepdims=True) + RMS_EPS) * w


def _silu(x):
    return x * jax.nn.sigmoid(x)


def _gelu(x):
    return 0.5 * x * (1.0 + lax.erf(x * 0.7071067811865476))


def _ffn_kernel(x_ref, nw_ref, wg_ref, wu_ref, wd_ref, fw_ref, o_ref, n_scr, a_scr, *, final_norm):
    n_scr[...] = _rms(x_ref[...], nw_ref[...]).astype(BF16)
    d_ff = wg_ref.shape[1]
    for j in range(d_ff // FFN_COLS):
        sl = slice(j * FFN_COLS, (j + 1) * FFN_COLS)
        n = n_scr[...]
        g = _dot(n, wg_ref[:, sl])
        u = _dot(n, wu_ref[:, sl])
        a_scr[:, sl] = (_silu(g) * u).astype(BF16)
    h = x_ref[...] + 0.5 * _dot(a_scr[...], wd_ref[...])
    if final_norm:
        h = _rms(h, fw_ref[...])
    o_ref[...] = h


def _ffn(x2d, norm_w, wg, wu, wd, final_w, final_norm):
    t, d = x2d.shape
    d_ff = wg.shape[1]
    tm = FFN_TOKENS
    row = pl.BlockSpec((tm, d), lambda i: (i, 0))
    return pl.pallas_call(
        functools.partial(_ffn_kernel, final_norm=final_norm),
        grid=(t // tm,),
        in_specs=[row, _resident((1, d)), _resident((d, d_ff)), _resident((d, d_ff)),
                  _resident((d_ff, d)), _resident((1, d))],
        out_specs=row,
        out_shape=jax.ShapeDtypeStruct((t, d), F32),
        scratch_shapes=[pltpu.VMEM((tm, d), BF16), pltpu.VMEM((tm, d_ff), BF16)],
        compiler_params=pltpu.CompilerParams(dimension_semantics=("arbitrary",),
                                             vmem_limit_bytes=VMEM_LIMIT),
        name="ffn_final" if final_norm else "ffn",
    )(x2d, norm_w, wg, wu, wd, final_w)


def _mix_in_kernel(h_ref, nw_ref, wqkv_ref, wg_ref, wba_ref, wu_ref, wv_ref, wga_ref, wgb_ref,
                   cw_ref, lnw_ref, lnb_ref,
                   q_ref, k_ref, v_ref, gs_ref, ba_ref, gu_ref, vln_ref, sga_ref, sgb_ref,
                   n_scr, conv_scr):
    ts = h_ref.shape[0]
    d = h_ref.shape[1]

    @pl.when(pl.program_id(1) == 0)
    def _():
        conv_scr[0:CARRY, :] = jnp.zeros((CARRY, conv_scr.shape[1]), F32)

    n_scr[...] = _rms(h_ref[...], nw_ref[...]).astype(BF16)

    n_heads = d // HEAD
    for j in range(3 * n_heads):
        sl = slice(j * HEAD, (j + 1) * HEAD)
        conv_scr[CARRY:CARRY + ts, sl] = _dot(n_scr[...], wqkv_ref[:, sl])
        acc = cw_ref[CONV_K - 1:CONV_K, sl] * conv_scr[CARRY:CARRY + ts, sl]
        for kk in range(CONV_K - 1):
            off = CARRY - (CONV_K - 1) + kk
            acc = acc + cw_ref[kk:kk + 1, sl] * conv_scr[off:off + ts, sl]
        y = _silu(acc)
        if j < 2 * n_heads:
            y = y * lax.rsqrt(jnp.sum(y * y, axis=-1, keepdims=True) + L2_EPS)
        dst = (q_ref, k_ref, v_ref)[j // n_heads]
        hs = slice((j % n_heads) * HEAD, (j % n_heads + 1) * HEAD)
        dst[:, hs] = y.astype(BF16)
    conv_scr[0:CARRY, :] = conv_scr[ts:ts + CARRY, :]

    n = n_scr[...]
    gs_ref[...] = _silu(_dot(n, wg_ref[...])).astype(BF16)
    ba_ref[...] = _dot(n, wba_ref[...])
    gu_ref[...] = _gelu(_dot(n, wu_ref[...])).astype(BF16)
    gv = _gelu(_dot(n, wv_ref[...]))
    mu = jnp.mean(gv, axis=-1, keepdims=True)
    xc = gv - mu
    ln = xc * lax.rsqrt(jnp.mean(xc * xc, axis=-1, keepdims=True) + LN_EPS)
    vln_ref[...] = (ln * lnw_ref[...] + lnb_ref[...]).astype(BF16)
    sga_ref[...] = jax.nn.sigmoid(_dot(n, wga_ref[...])).astype(BF16)
    sgb_ref[...] = jax.nn.sigmoid(_dot(n, wgb_ref[...])).astype(BF16)


def _mix_in(h, norm_w, wqkv, wg, wba, wu, wv, wga, wgb, conv_w, ln_w, ln_b):
    b, s, d = h.shape
    ts = MIX_TOKENS
    tile = lambda w: pl.BlockSpec((None, ts, w), lambda i, j: (i, j, 0))
    bf = lambda w: jax.ShapeDtypeStruct((b, s, w), BF16)
    return pl.pallas_call(
        _mix_in_kernel,
        grid=(b, s // ts),
        in_specs=[tile(d), _resident((1, d)), _resident(wqkv.shape), _resident(wg.shape),
                  _resident(wba.shape), _resident(wu.shape), _resident(wv.shape),
                  _resident(wga.shape), _resident(wgb.shape), _resident(conv_w.shape),
                  _resident((1, d)), _resident((1, d))],
        out_specs=[tile(d), tile(d), tile(d), tile(d), tile(LANE), tile(d), tile(d), tile(d), tile(d)],
        out_shape=[bf(d), bf(d), bf(d), bf(d), jax.ShapeDtypeStruct((b, s, LANE), F32),
                   bf(d), bf(d), bf(d), bf(d)],
        scratch_shapes=[pltpu.VMEM((ts, d), BF16), pltpu.VMEM((ts + CARRY, 3 * d), F32)],
        compiler_params=pltpu.CompilerParams(dimension_semantics=("arbitrary", "arbitrary"),
                                             vmem_limit_bytes=VMEM_LIMIT),
        name="mix_in",
    )(h, norm_w, wqkv, wg, wba, wu, wv, wga, wgb, conv_w, ln_w, ln_b)


def _delta_kernel(q_ref, k_ref, v_ref, ba_ref, gs_ref, hp_ref, onw_ref, o_ref, s_scr):
    ts, d = q_ref.shape
    n_heads = d // HEAD
    c = CHUNK

    @pl.when(pl.program_id(1) == 0)
    def _():
        s_scr[...] = jnp.zeros(s_scr.shape, F32)

    row = lax.broadcasted_iota(jnp.int32, (c, c), 0)
    col = lax.broadcasted_iota(jnp.int32, (c, c), 1)
    causal = row >= col
    strict = row > col
    tri = causal.astype(F32)
    eye = (row == col).astype(F32)
    scale = HEAD ** -0.5

    def chunk_body(ci, carry):
        r0 = pl.multiple_of(ci * c, c)
        rows = pl.ds(r0, c)
        ba = ba_ref[rows, :]
        beta_all = jax.nn.sigmoid(ba)
        z = ba + hp_ref[1:2, :]
        softplus = jnp.maximum(z, 0.0) + jnp.log1p(jnp.exp(-jnp.abs(z)))
        la = -jnp.exp(hp_ref[0:1, :]) * softplus
        g_all = jnp.dot(tri, la, preferred_element_type=F32, precision=lax.Precision.HIGHEST)
        g_t = g_all.T
        eg_all = jnp.exp(g_all)
        g_last_all = g_all[c - 1:c, :]
        ekd_all = jnp.exp(g_last_all - g_all)
        egl_all = jnp.exp(g_last_all)
        heads = range(n_heads)
        hsl = [slice(h * HEAD, (h + 1) * HEAD) for h in heads]
        qs = [q_ref[rows, hsl[h]] for h in heads]
        ks = [k_ref[rows, hsl[h]] for h in heads]
        qk_kk = [lax.dot_general(jnp.concatenate([qs[h], ks[h]], axis=0), ks[h],
                                 (((1,), (1,)), ((), ())), preferred_element_type=F32) for h in heads]
        beta, eg, qkm, ms, ps = [], [], [], [], []
        for h in heads:
            ga = n_heads + h
            beta.append(beta_all[:, h:h + 1])
            eg.append(eg_all[:, ga:ga + 1])
            gcol = g_all[:, ga:ga + 1]
            grow = g_t[ga:ga + 1, :]
            decay = jnp.where(causal, jnp.exp(jnp.where(causal, gcol - grow, 0.0)), 0.0)
            qkm.append((qk_kk[h][:c] * (decay * scale)).astype(BF16))
            lmat = jnp.where(strict, beta[h] * qk_kk[h][c:] * decay, 0.0)
            ms.append((-lmat).astype(BF16))
            ps.append(eye - lmat)
        sq = c
        while sq > 2:
            ms = [_dot(ms[h], ms[h]).astype(BF16) for h in heads]
            ps = [ps[h] + _dot(ms[h], ps[h].astype(BF16)) for h in heads]
            sq //= 2
        sol = []
        for h in heads:
            khf = ks[h].astype(F32)
            vh = v_ref[rows, hsl[h]].astype(F32)
            rhs = jnp.concatenate([vh * beta[h], khf * (beta[h] * eg[h])], axis=1).astype(BF16)
            sol.append(_dot(ps[h].astype(BF16), rhs))
        states = [s_scr[h] for h in heads]
        ws = []
        for h in heads:
            q_dec = qs[h].astype(F32) * (eg[h] * scale)
            lhs = jnp.concatenate([sol[h][:, HEAD:], q_dec], axis=0).astype(BF16)
            ws.append(_dot(lhs, states[h].astype(BF16)))
        for h in heads:
            ga = n_heads + h
            v_new = (sol[h][:, :HEAD] - ws[h][:c]).astype(BF16)
            k_dec = (ks[h].astype(F32) * ekd_all[:, ga:ga + 1]).astype(BF16)
            o = ws[h][c:] + _dot(qkm[h], v_new)
            ds = lax.dot_general(k_dec, v_new, (((0,), (0,)), ((), ())), preferred_element_type=F32)
            s_scr[h] = states[h] * egl_all[:, ga:ga + 1] + ds
            on = o * lax.rsqrt(jnp.mean(o * o, axis=-1, keepdims=True) + RMS_EPS) * onw_ref[...]
            o_ref[rows, hsl[h]] = (on * gs_ref[rows, hsl[h]].astype(F32)).astype(BF16)
        return carry

    lax.fori_loop(0, ts // c, chunk_body, 0)


def _delta(q, k, v, ba, gs, hp, onw):
    b, s, d = q.shape
    ts = DELTA_TOKENS
    tile = lambda w: pl.BlockSpec((None, ts, w), lambda i, j: (i, j, 0))
    return pl.pallas_call(
        _delta_kernel,
        grid=(b, s // ts),
        in_specs=[tile(d), tile(d), tile(d), tile(LANE), tile(d), _resident(hp.shape),
                  _resident(onw.shape)],
        out_specs=tile(d),
        out_shape=jax.ShapeDtypeStruct((b, s, d), BF16),
        scratch_shapes=[pltpu.VMEM((d // HEAD, HEAD, HEAD), F32)],
        compiler_params=pltpu.CompilerParams(dimension_semantics=("arbitrary", "arbitrary"),
                                             vmem_limit_bytes=VMEM_LIMIT),
        name="delta",
    )(q, k, v, ba, gs, hp, onw)


def _mix_out_kernel(h_ref, og_ref, gu_ref, vln_ref, sga_ref, sgb_ref, ws_ref, bs_ref,
                    wdo_ref, wgo_ref, wout_ref, o_ref, sg_scr):
    ts, d = h_ref.shape
    n_groups = d // HEAD
    gc = GM_CHUNK
    pos_r = lax.broadcasted_iota(jnp.int32, (gc, gc), 0) // CHUNK
    pos_c = lax.broadcasted_iota(jnp.int32, (gc, gc), 1) // CHUNK
    mask = pos_c <= pos_r
    for h in range(n_groups):
        hs = slice(h * HEAD, (h + 1) * HEAD)
        w_m = jnp.where(mask, ws_ref[h], 0.0).astype(BF16)
        bias = bs_ref[:, h:h + 1]
        for g in range(ts // gc):
            rows = slice(g * gc, (g + 1) * gc)
            mixed = _dot(w_m, vln_ref[rows, hs]) + bias
            sg_scr[rows, hs] = (gu_ref[rows, hs].astype(F32) * mixed).astype(BF16)
    y_a = _dot(og_ref[...], wdo_ref[...])
    y_b = _dot(sg_scr[...], wgo_ref[...])
    merged = sga_ref[...].astype(F32) * y_a + sgb_ref[...].astype(F32) * y_b
    o_ref[...] = h_ref[...] + _dot(merged.astype(BF16), wout_ref[...])


def _mix_out(h, og, gu, vln, sga, sgb, w_s, b_s_t, wdo, wgo, wout):
    b, s, d = h.shape
    ts = MIX_TOKENS
    tile = pl.BlockSpec((None, ts, d), lambda i, j: (i, j, 0))
    return pl.pallas_call(
        _mix_out_kernel,
        grid=(b, s // ts),
        in_specs=[tile] * 6 + [_resident(w_s.shape), _resident(b_s_t.shape), _resident(wdo.shape),
                               _resident(wgo.shape), _resident(wout.shape)],
        out_specs=tile,
        out_shape=jax.ShapeDtypeStruct((b, s, d), F32),
        scratch_shapes=[pltpu.VMEM((ts, d), BF16)],
        compiler_params=pltpu.CompilerParams(dimension_semantics=("arbitrary", "arbitrary"),
                                             vmem_limit_bytes=VMEM_LIMIT),
        name="mix_out",
    )(h, og, gu, vln, sga, sgb, w_s, b_s_t, wdo, wgo, wout)


def _layer(h, p):
    b, s, d = h.shape
    n_heads = d // HEAD
    row = lambda a: a.reshape(1, -1).astype(F32)
    bf = lambda a: a.astype(BF16)
    unused_w = jnp.ones((1, d), F32)

    h1 = _ffn(h.reshape(b * s, d), row(p["ffn1_norm_w"]), bf(p["ffn1_w_gate"]), bf(p["ffn1_w_up"]),
              bf(p["ffn1_w_down"]), unused_w, False).reshape(b, s, d)

    w_in = p["w_in"]
    bounds = [0, 3 * d, 4 * d, 4 * d + 2 * n_heads]
    bounds += [bounds[-1] + d * i for i in range(1, 5)]
    wqkv, wg, wba, wu, wv, wga, wgb = (w_in[:, lo:hi] for lo, hi in zip(bounds[:-1], bounds[1:]))
    wba = jnp.pad(wba, ((0, 0), (0, LANE - 2 * n_heads)))
    q, k, v, gs, ba, gu, vln, sga, sgb = _mix_in(
        h1, row(p["mix_norm_w"]), bf(wqkv), bf(wg), bf(wba), bf(wu), bf(wv), bf(wga), bf(wgb),
        p["dn_conv_w"].astype(F32), row(p["sgu_norm_w"]), row(p["sgu_norm_b"]))

    hp = jnp.zeros((2, LANE), F32)
    hp = hp.at[0, n_heads:2 * n_heads].set(p["dn_a_log"].astype(F32))
    hp = hp.at[1, n_heads:2 * n_heads].set(p["dn_dt_bias"].astype(F32))
    og = _delta(q, k, v, ba, gs, hp, row(p["dn_out_norm_w"]))

    h2 = _mix_out(h1, og, gu, vln, sga, sgb, p["sgu_w_s"].astype(F32),
                  jnp.swapaxes(p["sgu_b"], 0, 1).astype(F32),
                  bf(p["dn_w_o"]), bf(p["gmlp_w_o"]), bf(p["w_out"]))
    return h2


def kernel(x, ffn1_norm_w, ffn1_w_gate, ffn1_w_up, ffn1_w_down, mix_norm_w, w_in, dn_conv_w, dn_a_log, dn_dt_bias, dn_out_norm_w, dn_w_o, sgu_norm_w, sgu_norm_b, sgu_w_s, sgu_b, gmlp_w_o, w_out, ffn2_norm_w, ffn2_w_gate, ffn2_w_up, ffn2_w_down, final_norm_w):
    depth = ffn1_norm_w.shape[0]
    assert depth >= 1
    b, s, d = x.shape
    h = x
    for l in range(depth):
        p = dict(ffn1_norm_w=ffn1_norm_w[l], ffn1_w_gate=ffn1_w_gate[l], ffn1_w_up=ffn1_w_up[l],
                 ffn1_w_down=ffn1_w_down[l], mix_norm_w=mix_norm_w[l], w_in=w_in[l],
                 dn_conv_w=dn_conv_w[l], dn_a_log=dn_a_log[l], dn_dt_bias=dn_dt_bias[l],
                 dn_out_norm_w=dn_out_norm_w[l], dn_w_o=dn_w_o[l], sgu_norm_w=sgu_norm_w[l],
                 sgu_norm_b=sgu_norm_b[l], sgu_w_s=sgu_w_s[l], sgu_b=sgu_b[l], gmlp_w_o=gmlp_w_o[l],
                 w_out=w_out[l])
        h2 = _layer(h, p)
        last = l == depth - 1
        bf = lambda a: a.astype(BF16)
        h = _ffn(h2.reshape(b * s, d), ffn2_norm_w[l].reshape(1, d), bf(ffn2_w_gate[l]),
                 bf(ffn2_w_up[l]), bf(ffn2_w_down[l]), final_norm_w.reshape(1, d),
                 last).reshape(b, s, d)
    return h
```

```python
import functools

import jax
import jax.numpy as jnp
from jax import lax
from jax.experimental import pallas as pl
from jax.experimental.pallas import tpu as pltpu

F32 = jnp.float32
BF16 = jnp.bfloat16

RMS_EPS = 1e-6
LN_EPS = 1e-5
L2_EPS = 1e-6
CHUNK = 64
GM_CHUNK = 128
HEAD = 128
CONV_K = 4
LANE = 128
CARRY = 8

FFN_TOKENS = 512
FFN_COLS = 256
MIX_TOKENS = 256
DELTA_TOKENS = 512
DELTA_ROWS = 2
DELTA_CHUNKS = 4
VMEM_LIMIT = 56 * 1024 * 1024


def _resident(shape):
    nd = len(shape)
    return pl.BlockSpec(shape, lambda *_: (0,) * nd, pipeline_mode=pl.Buffered(1))


def _dot(a, b):
    return jnp.dot(a, b, preferred_element_type=F32)


def _rms(x, w):
    return x * lax.rsqrt(jnp.mean(x * x, axis=-1, keepdims=True) + RMS_EPS) * w


def _silu(x):
    return x * jax.nn.sigmoid(x)


def _gelu(x):
    return 0.5 * x * (1.0 + lax.erf(x * 0.7071067811865476))


def _ffn_kernel(x_ref, nw_ref, wg_ref, wu_ref, wd_ref, fw_ref, o_ref, n_scr, a_scr, *, final_norm):
    n_scr[...] = _rms(x_ref[...], nw_ref[...]).astype(BF16)
    d_ff = wg_ref.shape[1]
    for j in range(d_ff // FFN_COLS):
        sl = slice(j * FFN_COLS, (j + 1) * FFN_COLS)
        n = n_scr[...]
        g = _dot(n, wg_ref[:, sl])
        u = _dot(n, wu_ref[:, sl])
        a_scr[:, sl] = (_silu(g) * u).astype(BF16)
    h = x_ref[...] + 0.5 * _dot(a_scr[...], wd_ref[...])
    if final_norm:
        h = _rms(h, fw_ref[...])
    o_ref[...] = h


def _ffn(x2d, norm_w, wg, wu, wd, final_w, final_norm):
    t, d = x2d.shape
    d_ff = wg.shape[1]
    tm = FFN_TOKENS
    row = pl.BlockSpec((tm, d), lambda i: (i, 0))
    return pl.pallas_call(
        functools.partial(_ffn_kernel, final_norm=final_norm),
        grid=(t // tm,),
        in_specs=[row, _resident((1, d)), _resident((d, d_ff)), _resident((d, d_ff)),
                  _resident((d_ff, d)), _resident((1, d))],
        out_specs=row,
        out_shape=jax.ShapeDtypeStruct((t, d), F32),
        scratch_shapes=[pltpu.VMEM((tm, d), BF16), pltpu.VMEM((tm, d_ff), BF16)],
        compiler_params=pltpu.CompilerParams(dimension_semantics=("arbitrary",),
                                             vmem_limit_bytes=VMEM_LIMIT),
        name="ffn_final" if final_norm else "ffn",
    )(x2d, norm_w, wg, wu, wd, final_w)


def _mix_in_kernel(h_ref, nw_ref, wqkv_ref, wg_ref, wba_ref, wu_ref, wv_ref, wga_ref, wgb_ref,
                   cw_ref, lnw_ref, lnb_ref,
                   q_ref, k_ref, v_ref, gs_ref, ba_ref, gu_ref, vln_ref, sga_ref, sgb_ref,
                   n_scr, conv_scr):
    ts = h_ref.shape[0]
    d = h_ref.shape[1]

    @pl.when(pl.program_id(1) == 0)
    def _():
        conv_scr[0:CARRY, :] = jnp.zeros((CARRY, conv_scr.shape[1]), F32)

    n_scr[...] = _rms(h_ref[...], nw_ref[...]).astype(BF16)

    n_heads = d // HEAD
    for j in range(3 * n_heads):
        sl = slice(j * HEAD, (j + 1) * HEAD)
        conv_scr[CARRY:CARRY + ts, sl] = _dot(n_scr[...], wqkv_ref[:, sl])
        acc = cw_ref[CONV_K - 1:CONV_K, sl] * conv_scr[CARRY:CARRY + ts, sl]
        for kk in range(CONV_K - 1):
            off = CARRY - (CONV_K - 1) + kk
            acc = acc + cw_ref[kk:kk + 1, sl] * conv_scr[off:off + ts, sl]
        y = _silu(acc)
        if j < 2 * n_heads:
            y = y * lax.rsqrt(jnp.sum(y * y, axis=-1, keepdims=True) + L2_EPS)
        dst = (q_ref, k_ref, v_ref)[j // n_heads]
        hs = slice((j % n_heads) * HEAD, (j % n_heads + 1) * HEAD)
        dst[:, hs] = y.astype(BF16)
    conv_scr[0:CARRY, :] = conv_scr[ts:ts + CARRY, :]

    n = n_scr[...]
    gs_ref[...] = _silu(_dot(n, wg_ref[...])).astype(BF16)
    ba_ref[...] = _dot(n, wba_ref[...])
    gu_ref[...] = _gelu(_dot(n, wu_ref[...])).astype(BF16)
    gv = _gelu(_dot(n, wv_ref[...]))
    mu = jnp.mean(gv, axis=-1, keepdims=True)
    xc = gv - mu
    ln = xc * lax.rsqrt(jnp.mean(xc * xc, axis=-1, keepdims=True) + LN_EPS)
    vln_ref[...] = (ln * lnw_ref[...] + lnb_ref[...]).astype(BF16)
    sga_ref[...] = jax.nn.sigmoid(_dot(n, wga_ref[...])).astype(BF16)
    sgb_ref[...] = jax.nn.sigmoid(_dot(n, wgb_ref[...])).astype(BF16)


def _mix_in(h, norm_w, wqkv, wg, wba, wu, wv, wga, wgb, conv_w, ln_w, ln_b):
    b, s, d = h.shape
    ts = MIX_TOKENS
    tile = lambda w: pl.BlockSpec((None, ts, w), lambda i, j: (i, j, 0))
    bf = lambda w: jax.ShapeDtypeStruct((b, s, w), BF16)
    return pl.pallas_call(
        _mix_in_kernel,
        grid=(b, s // ts),
        in_specs=[tile(d), _resident((1, d)), _resident(wqkv.shape), _resident(wg.shape),
                  _resident(wba.shape), _resident(wu.shape), _resident(wv.shape),
                  _resident(wga.shape), _resident(wgb.shape), _resident(conv_w.shape),
                  _resident((1, d)), _resident((1, d))],
        out_specs=[tile(d), tile(d), tile(d), tile(d), tile(LANE), tile(d), tile(d), tile(d), tile(d)],
        out_shape=[bf(d), bf(d), bf(d), bf(d), jax.ShapeDtypeStruct((b, s, LANE), F32),
                   bf(d), bf(d), bf(d), bf(d)],
        scratch_shapes=[pltpu.VMEM((ts, d), BF16), pltpu.VMEM((ts + CARRY, 3 * d), F32)],
        compiler_params=pltpu.CompilerParams(dimension_semantics=("arbitrary", "arbitrary"),
                                             vmem_limit_bytes=VMEM_LIMIT),
        name="mix_in",
    )(h, norm_w, wqkv, wg, wba, wu, wv, wga, wgb, conv_w, ln_w, ln_b)


def _delta_kernel(q_ref, k_ref, v_ref, ba_ref, gs_ref, hp_ref, onw_ref, o_ref, s_scr):
    nr, ts, d = q_ref.shape
    n_heads = d // HEAD
    c = CHUNK
    nc = DELTA_CHUNKS
    gh = 2 * LANE // c
    n_groups = n_heads // gh
    heads = range(n_heads)
    groups = range(n_groups)
    hsl = [slice(h * HEAD, (h + 1) * HEAD) for h in heads]

    @pl.when(pl.program_id(1) == 0)
    def _():
        s_scr[...] = jnp.zeros(s_scr.shape, F32)

    row = lax.broadcasted_iota(jnp.int32, (c, c), 0)
    col = lax.broadcasted_iota(jnp.int32, (c, c), 1)
    tri = (row >= col).astype(F32)
    prow = lax.broadcasted_iota(jnp.int32, (c, gh * c), 0)
    plane = lax.broadcasted_iota(jnp.int32, (c, gh * c), 1)
    pcol = plane % c
    causal = prow >= pcol
    strict = prow > pcol
    diag = prow == pcol
    eye = diag.astype(F32)
    head_of_lane = plane // c
    low_half = lax.broadcasted_iota(jnp.int32, (c, LANE), 1) < c
    scale = HEAD ** -0.5
    zero_blk = jnp.zeros((c, HEAD), BF16)
    zero_state = jnp.zeros((HEAD, HEAD), BF16)
    nt = (((1,), (1,)), ((), ()))

    def block_diag(x):
        return jnp.concatenate([jnp.where(head_of_lane == h, x, jnp.zeros_like(x)) for h in range(gh)],
                               axis=0)

    def lane_blocks(block, h, zero):
        return jnp.concatenate([block if j == h else zero for j in range(gh)], axis=1)

    def pack(cols):
        return jnp.concatenate([jnp.where(low_half, cols[2 * i], cols[2 * i + 1])
                                for i in range(gh // 2)], axis=1)

    def decay_terms(r, rows):
        ba = ba_ref[r, rows, :]
        z = ba + hp_ref[1:2, :]
        softplus = jnp.maximum(z, 0.0) + jnp.log1p(jnp.exp(-jnp.abs(z)))
        la = -jnp.exp(hp_ref[0:1, :]) * softplus
        g_all = jnp.dot(tri, la, preferred_element_type=F32, precision=lax.Precision.HIGHEST)
        g_last = g_all[c - 1:c, :]
        bc = lambda x, lane: jnp.broadcast_to(x[:, lane:lane + 1], (c, LANE))
        eg_all = jnp.exp(g_all)
        ekd_all = jnp.exp(g_last - g_all)
        return dict(beta=[bc(jax.nn.sigmoid(ba), h) for h in heads],
                    g=[bc(g_all, n_heads + h) for h in heads],
                    eg=[bc(eg_all, n_heads + h) for h in heads],
                    ekd=[bc(ekd_all, n_heads + h) for h in heads],
                    egl=jnp.exp(g_last))

    def block_body(bi, carry):
        items = [(r, cc) for cc in range(nc) for r in range(nr)]
        rows_of = {cc: pl.ds(pl.multiple_of((bi * nc + cc) * c, c), c) for cc in range(nc)}
        streams = [(it, g) for it in items for g in groups]
        t = {it: decay_terms(it[0], rows_of[it[1]]) for it in items}
        kh = {(it, h): k_ref[it[0], rows_of[it[1]], hsl[h]] for it in items for h in heads}

        qkm, ms, ps = {}, {}, {}
        for st in streams:
            (r, cc), g = st
            hs = range(g * gh, (g + 1) * gh)
            gsl = slice(g * gh * HEAD, (g + 1) * gh * HEAD)
            lhs = jnp.concatenate([q_ref[r, rows_of[cc], gsl], k_ref[r, rows_of[cc], gsl]], axis=0)
            bd_k = jnp.concatenate([lane_blocks(kh[(r, cc), h], h - g * gh, zero_blk) for h in hs], axis=0)
            qk_kk = lax.dot_general(lhs, bd_k, nt, preferred_element_type=F32)
            gcol = pack([t[r, cc]["g"][h] for h in hs])
            grow = jnp.sum(jnp.where(diag, gcol, 0.0), axis=0, keepdims=True)
            decay = jnp.where(causal, jnp.exp(jnp.where(causal, gcol - grow, 0.0)), 0.0)
            qkm[st] = (qk_kk[:c] * (decay * scale)).astype(BF16)
            lmat = jnp.where(strict, pack([t[r, cc]["beta"][h] for h in hs]) * qk_kk[c:] * decay, 0.0)
            ms[st] = (-lmat).astype(BF16)
            ps[st] = eye - lmat
        ms = {st: _dot(ms[st], block_diag(ms[st])).astype(BF16) for st in streams}
        sq = c // 2
        while sq > 2:
            prod = {st: _dot(jnp.concatenate([ms[st], ps[st].astype(BF16)], axis=0), block_diag(ms[st]))
                    for st in streams}
            ms = {st: prod[st][:c].astype(BF16) for st in streams}
            ps = {st: ps[st] + prod[st][c:] for st in streams}
            sq //= 2
        ps = {st: ps[st] + _dot(ps[st].astype(BF16), block_diag(ms[st])) for st in streams}
        sols = {}
        for st in streams:
            (r, cc), g = st
            rhs = []
            for h in range(g * gh, (g + 1) * gh):
                beta, eg = t[r, cc]["beta"][h], t[r, cc]["eg"][h]
                vh = v_ref[r, rows_of[cc], hsl[h]].astype(F32)
                rhs.append(jnp.concatenate([vh * beta, kh[(r, cc), h].astype(F32) * (beta * eg)], axis=1))
            rhs = jnp.concatenate(rhs, axis=0).astype(BF16)
            sols[st] = _dot(block_diag(ps[st].astype(BF16)), rhs)

        def sol_rows(it, h):
            return sols[it, h // gh][(h % gh) * c:(h % gh + 1) * c]

        states = {(r, h): s_scr[r, h] for r in range(nr) for h in heads}
        for cc in range(nc):
            rows = rows_of[cc]
            ws = {}
            for r in range(nr):
                for a in range(n_heads // 2):
                    pair = (2 * a, 2 * a + 1)
                    w_row = jnp.concatenate([sol_rows((r, cc), h)[:, HEAD:] for h in pair], axis=1)
                    qd_row = jnp.concatenate(
                        [q_ref[r, rows, hsl[h]].astype(F32) * (t[r, cc]["eg"][h] * scale) for h in pair],
                        axis=1)
                    lhs = jnp.concatenate([w_row, qd_row], axis=0).astype(BF16)
                    s0, s1 = (states[r, h].astype(BF16) for h in pair)
                    rhs = jnp.concatenate([jnp.concatenate([s0, zero_state], axis=1),
                                           jnp.concatenate([zero_state, s1], axis=1)], axis=0)
                    ws[r, a] = _dot(lhs, rhs)

            def ws_of(r, h):
                return ws[r, h // 2][:, (h % 2) * HEAD:(h % 2 + 1) * HEAD]

            for r in range(nr):
                for g in groups:
                    hs = range(g * gh, (g + 1) * gh)
                    vn = [(sol_rows((r, cc), h)[:, :HEAD] - ws_of(r, h)[:c]).astype(BF16) for h in hs]
                    rhs = jnp.concatenate([lane_blocks(vn[i], i, zero_blk) for i in range(gh)], axis=0)
                    kd = [kh[(r, cc), h].astype(F32) * t[r, cc]["ekd"][h] for h in hs]
                    kdt = jnp.concatenate([jnp.concatenate([kd[2 * i], kd[2 * i + 1]], axis=0).T
                                           for i in range(gh // 2)], axis=1)
                    lhs = jnp.concatenate([qkm[(r, cc), g], kdt.astype(BF16)], axis=0)
                    out = _dot(lhs, rhs)
                    for i, h in enumerate(hs):
                        osl = slice(i * HEAD, (i + 1) * HEAD)
                        o = ws_of(r, h)[c:] + out[:c, osl]
                        gl = t[r, cc]["egl"][:, n_heads + h:n_heads + h + 1]
                        states[r, h] = states[r, h] * gl + out[c:, osl]
                        on = o * lax.rsqrt(jnp.mean(o * o, axis=-1, keepdims=True) + RMS_EPS) * onw_ref[...]
                        o_ref[r, rows, hsl[h]] = (on * gs_ref[r, rows, hsl[h]].astype(F32)).astype(BF16)
        for r in range(nr):
            for h in heads:
                s_scr[r, h] = states[r, h]
        return carry

    lax.fori_loop(0, ts // (c * nc), block_body, 0)


def _delta(q, k, v, ba, gs, hp, onw):
    b, s, d = q.shape
    assert HEAD == LANE and 2 * LANE % CHUNK == 0 and (d // HEAD) % (2 * LANE // CHUNK) == 0
    ts, nr = DELTA_TOKENS, DELTA_ROWS
    tile = lambda w: pl.BlockSpec((nr, ts, w), lambda i, j: (i, j, 0))
    return pl.pallas_call(
        _delta_kernel,
        grid=(b // nr, s // ts),
        in_specs=[tile(d), tile(d), tile(d), tile(LANE), tile(d), _resident(hp.shape),
                  _resident(onw.shape)],
        out_specs=tile(d),
        out_shape=jax.ShapeDtypeStruct((b, s, d), BF16),
        scratch_shapes=[pltpu.VMEM((nr, d // HEAD, HEAD, HEAD), F32)],
        compiler_params=pltpu.CompilerParams(dimension_semantics=("arbitrary", "arbitrary"),
                                             vmem_limit_bytes=VMEM_LIMIT),
        name="delta",
    )(q, k, v, ba, gs, hp, onw)


def _mix_out_kernel(h_ref, og_ref, gu_ref, vln_ref, sga_ref, sgb_ref, ws_ref, bs_ref,
                    wdo_ref, wgo_ref, wout_ref, o_ref, sg_scr):
    ts, d = h_ref.shape
    n_groups = d // HEAD
    gc = GM_CHUNK
    pos_r = lax.broadcasted_iota(jnp.int32, (gc, gc), 0) // CHUNK
    pos_c = lax.broadcasted_iota(jnp.int32, (gc, gc), 1) // CHUNK
    mask = pos_c <= pos_r
    for h in range(n_groups):
        hs = slice(h * HEAD, (h + 1) * HEAD)
        w_m = jnp.where(mask, ws_ref[h], 0.0).astype(BF16)
        bias = bs_ref[:, h:h + 1]
        for g in range(ts // gc):
            rows = slice(g * gc, (g + 1) * gc)
            mixed = _dot(w_m, vln_ref[rows, hs]) + bias
            sg_scr[rows, hs] = (gu_ref[rows, hs].astype(F32) * mixed).astype(BF16)
    y_a = _dot(og_ref[...], wdo_ref[...])
    y_b = _dot(sg_scr[...], wgo_ref[...])
    merged = sga_ref[...].astype(F32) * y_a + sgb_ref[...].astype(F32) * y_b
    o_ref[...] = h_ref[...] + _dot(merged.astype(BF16), wout_ref[...])


def _mix_out(h, og, gu, vln, sga, sgb, w_s, b_s_t, wdo, wgo, wout):
    b, s, d = h.shape
    ts = MIX_TOKENS
    tile = pl.BlockSpec((None, ts, d), lambda i, j: (i, j, 0))
    return pl.pallas_call(
        _mix_out_kernel,
        grid=(b, s // ts),
        in_specs=[tile] * 6 + [_resident(w_s.shape), _resident(b_s_t.shape), _resident(wdo.shape),
                               _resident(wgo.shape), _resident(wout.shape)],
        out_specs=tile,
        out_shape=jax.ShapeDtypeStruct((b, s, d), F32),
        scratch_shapes=[pltpu.VMEM((ts, d), BF16)],
        compiler_params=pltpu.CompilerParams(dimension_semantics=("arbitrary", "arbitrary"),
                                             vmem_limit_bytes=VMEM_LIMIT),
        name="mix_out",
    )(h, og, gu, vln, sga, sgb, w_s, b_s_t, wdo, wgo, wout)


def _layer(h, p):
    b, s, d = h.shape
    n_heads = d // HEAD
    row = lambda a: a.reshape(1, -1).astype(F32)
    bf = lambda a: a.astype(BF16)
    unused_w = jnp.ones((1, d), F32)

    h1 = _ffn(h.reshape(b * s, d), row(p["ffn1_norm_w"]), bf(p["ffn1_w_gate"]), bf(p["ffn1_w_up"]),
              bf(p["ffn1_w_down"]), unused_w, False).reshape(b, s, d)

    w_in = p["w_in"]
    bounds = [0, 3 * d, 4 * d, 4 * d + 2 * n_heads]
    bounds += [bounds[-1] + d * i for i in range(1, 5)]
    wqkv, wg, wba, wu, wv, wga, wgb = (w_in[:, lo:hi] for lo, hi in zip(bounds[:-1], bounds[1:]))
    wba = jnp.pad(wba, ((0, 0), (0, LANE - 2 * n_heads)))
    q, k, v, gs, ba, gu, vln, sga, sgb = _mix_in(
        h1, row(p["mix_norm_w"]), bf(wqkv), bf(wg), bf(wba), bf(wu), bf(wv), bf(wga), bf(wgb),
        p["dn_conv_w"].astype(F32), row(p["sgu_norm_w"]), row(p["sgu_norm_b"]))

    hp = jnp.zeros((2, LANE), F32)
    hp = hp.at[0, n_heads:2 * n_heads].set(p["dn_a_log"].astype(F32))
    hp = hp.at[1, n_heads:2 * n_heads].set(p["dn_dt_bias"].astype(F32))
    og = _delta(q, k, v, ba, gs, hp, row(p["dn_out_norm_w"]))

    h2 = _mix_out(h1, og, gu, vln, sga, sgb, p["sgu_w_s"].astype(F32),
                  jnp.swapaxes(p["sgu_b"], 0, 1).astype(F32),
                  bf(p["dn_w_o"]), bf(p["gmlp_w_o"]), bf(p["w_out"]))
    return h2


def kernel(x, ffn1_norm_w, ffn1_w_gate, ffn1_w_up, ffn1_w_down, mix_norm_w, w_in, dn_conv_w, dn_a_log, dn_dt_bias, dn_out_norm_w, dn_w_o, sgu_norm_w, sgu_norm_b, sgu_w_s, sgu_b, gmlp_w_o, w_out, ffn2_norm_w, ffn2_w_gate, ffn2_w_up, ffn2_w_down, final_norm_w):
    depth = ffn1_norm_w.shape[0]
    assert depth >= 1
    b, s, d = x.shape
    h = x
    for l in range(depth):
        p = dict(ffn1_norm_w=ffn1_norm_w[l], ffn1_w_gate=ffn1_w_gate[l], ffn1_w_up=ffn1_w_up[l],
                 ffn1_w_down=ffn1_w_down[l], mix_norm_w=mix_norm_w[l], w_in=w_in[l],
                 dn_conv_w=dn_conv_w[l], dn_a_log=dn_a_log[l], dn_dt_bias=dn_dt_bias[l],
                 dn_out_norm_w=dn_out_norm_w[l], dn_w_o=dn_w_o[l], sgu_norm_w=sgu_norm_w[l],
                 sgu_norm_b=sgu_norm_b[l], sgu_w_s=sgu_w_s[l], sgu_b=sgu_b[l], gmlp_w_o=gmlp_w_o[l],
                 w_out=w_out[l])
        h2 = _layer(h, p)
        last = l == depth - 1
        bf = lambda a: a.astype(BF16)
        h = _ffn(h2.reshape(b * s, d), ffn2_norm_w[l].reshape(1, d), bf(ffn2_w_gate[l]),
                 bf(ffn2_w_up[l]), bf(ffn2_w_down[l]), final_norm_w.reshape(1, d),
                 last).reshape(b, s, d)
    return h
```

```python
import functools

import jax
import jax.numpy as jnp
from jax import lax
from jax.experimental import pallas as pl
from jax.experimental.pallas import tpu as pltpu

F32 = jnp.float32
BF16 = jnp.bfloat16

RMS_EPS = 1e-6
LN_EPS = 1e-5
L2_EPS = 1e-6
CHUNK = 64
GM_CHUNK = 128
HEAD = 128
CONV_K = 4
LANE = 128
CARRY = 8

FFN_TOKENS = 512
FFN_COLS = 256
MIX_IN_TOKENS = 256
MIX_OUT_TOKENS = 512
CONV_COLS = 256
CONV_ROWS = 256
PROJ_COLS = 1024
DELTA_TOKENS = 512
DELTA_ROWS = 2
DELTA_CHUNKS = 4
VMEM_LIMIT = 56 * 1024 * 1024


def _resident(shape):
    nd = len(shape)
    return pl.BlockSpec(shape, lambda *_: (0,) * nd, pipeline_mode=pl.Buffered(1))


def _pad_cols(w, used):
    tiles = -(-used // LANE)
    tiles += 1 - tiles % 2
    return jnp.pad(w, ((0, 0), (0, tiles * LANE - w.shape[1])))


def _dot(a, b):
    return jnp.dot(a, b, preferred_element_type=F32)


def _rms(x, w):
    return x * lax.rsqrt(jnp.mean(x * x, axis=-1, keepdims=True) + RMS_EPS) * w


def _silu(x):
    return x * jax.nn.sigmoid(x)


def _gelu(x):
    return 0.5 * x * (1.0 + lax.erf(x * 0.7071067811865476))


def _ffn_kernel(x_ref, nw_ref, wg_ref, wu_ref, wd_ref, ow_ref, *rest, final_norm):
    if final_norm:
        o_ref, n_scr, a_scr = rest
    else:
        o_ref, nn_ref, n_scr, a_scr = rest
    n_scr[...] = _rms(x_ref[...], nw_ref[...]).astype(BF16)
    d_ff = wg_ref.shape[1]
    for j in range(d_ff // FFN_COLS):
        sl = slice(j * FFN_COLS, (j + 1) * FFN_COLS)
        n = n_scr[...]
        g = _dot(n, wg_ref[:, sl])
        u = _dot(n, wu_ref[:, sl])
        a_scr[:, sl] = (_silu(g) * u).astype(BF16)
    h = x_ref[...] + 0.5 * _dot(a_scr[...], wd_ref[:, 0:x_ref.shape[1]])
    normed = _rms(h, ow_ref[...])
    if final_norm:
        o_ref[...] = normed
    else:
        o_ref[...] = h
        nn_ref[...] = normed.astype(BF16)


def _ffn(x2d, norm_w, wg, wu, wd, out_norm_w, final_norm):
    t, d = x2d.shape
    d_ff = wg.shape[1]
    tm = FFN_TOKENS
    wd = _pad_cols(wd, d)
    row = pl.BlockSpec((tm, d), lambda i: (i, 0))
    out_shape = [jax.ShapeDtypeStruct((t, d), F32)]
    if not final_norm:
        out_shape.append(jax.ShapeDtypeStruct((t, d), BF16))
    return pl.pallas_call(
        functools.partial(_ffn_kernel, final_norm=final_norm),
        grid=(t // tm,),
        in_specs=[row, _resident((1, d)), _resident((d, d_ff)), _resident((d, d_ff)),
                  _resident(wd.shape), _resident((1, d))],
        out_specs=[row] * len(out_shape),
        out_shape=out_shape,
        scratch_shapes=[pltpu.VMEM((tm, d), BF16), pltpu.VMEM((tm, d_ff), BF16)],
        compiler_params=pltpu.CompilerParams(dimension_semantics=("arbitrary",),
                                             vmem_limit_bytes=VMEM_LIMIT),
        name="ffn_final" if final_norm else "ffn",
    )(x2d, norm_w, wg, wu, wd, out_norm_w)


def _mix_in_kernel(n_ref, w_ref, cw_ref, q_ref, k_ref, v_ref, gs_ref, ba_ref, carry_scr):
    ts, d = n_ref.shape

    @pl.when(pl.program_id(1) == 0)
    def _():
        carry_scr[...] = jnp.zeros(carry_scr.shape, F32)

    assert CONV_K == 4 and cw_ref.shape[0] == CONV_K
    n_heads = d // HEAD
    for j in range(3 * d // CONV_COLS):
        sl = slice(j * CONV_COLS, (j + 1) * CONV_COLS)
        if j % (PROJ_COLS // CONV_COLS) == 0:
            proj = _dot(n_ref[...], w_ref[:, j * CONV_COLS:j * CONV_COLS + PROJ_COLS])
        off = (j * CONV_COLS) % PROJ_COLS
        w0, w1, w2, w3 = (cw_ref[kk:kk + 1, sl] for kk in range(CONV_K))
        prev = carry_scr[:, sl]
        carry_scr[:, sl] = proj[ts - CARRY:, off:off + CONV_COLS]
        for r0 in range(0, ts, CONV_ROWS):
            rows = slice(r0, r0 + CONV_ROWS)
            x = proj[rows, off:off + CONV_COLS]
            xe = jnp.concatenate([prev, x], axis=0)
            prev = x[CONV_ROWS - CARRY:, :]
            x2 = pltpu.roll(xe, 2, axis=0)
            acc = w3 * xe + w1 * x2 + pltpu.roll(w2 * xe + w0 * x2, 1, axis=0)
            y = _silu(acc[CARRY:, :])
            for i in range(CONV_COLS // HEAD):
                head = j * (CONV_COLS // HEAD) + i
                yh = y[:, i * HEAD:(i + 1) * HEAD]
                if head < 2 * n_heads:
                    yh = yh * lax.rsqrt(jnp.sum(yh * yh, axis=-1, keepdims=True) + L2_EPS)
                dst = (q_ref, k_ref, v_ref)[head // n_heads]
                hs = slice((head % n_heads) * HEAD, (head % n_heads + 1) * HEAD)
                dst[rows, hs] = yh.astype(BF16)

    n = n_ref[...]
    gs_ref[...] = _silu(_dot(n, w_ref[:, 3 * d:4 * d])).astype(BF16)
    ba_ref[...] = _dot(n, w_ref[:, 4 * d:4 * d + LANE])


def _mix_in(n, w, conv_w):
    b, s, d = n.shape
    ts = MIX_IN_TOKENS
    tile = lambda w: pl.BlockSpec((None, ts, w), lambda i, j: (i, j, 0))
    bf = lambda w: jax.ShapeDtypeStruct((b, s, w), BF16)
    return pl.pallas_call(
        _mix_in_kernel,
        grid=(b, s // ts),
        in_specs=[tile(d), _resident(w.shape), _resident(conv_w.shape)],
        out_specs=[tile(d), tile(d), tile(d), tile(d), tile(LANE)],
        out_shape=[bf(d), bf(d), bf(d), bf(d), jax.ShapeDtypeStruct((b, s, LANE), F32)],
        scratch_shapes=[pltpu.VMEM((CARRY, 3 * d), F32)],
        compiler_params=pltpu.CompilerParams(dimension_semantics=("arbitrary", "arbitrary"),
                                             vmem_limit_bytes=VMEM_LIMIT),
        name="mix_in",
    )(n, w, conv_w)


def _delta_kernel(q_ref, k_ref, v_ref, ba_ref, gs_ref, hp_ref, onw_ref, o_ref, s_scr):
    nr, ts, d = q_ref.shape
    n_heads = d // HEAD
    c = CHUNK
    nc = DELTA_CHUNKS
    gh = 2 * LANE // c
    n_groups = n_heads // gh
    heads = range(n_heads)
    groups = range(n_groups)
    hsl = [slice(h * HEAD, (h + 1) * HEAD) for h in heads]

    @pl.when(pl.program_id(1) == 0)
    def _():
        s_scr[...] = jnp.zeros(s_scr.shape, F32)

    row = lax.broadcasted_iota(jnp.int32, (c, c), 0)
    col = lax.broadcasted_iota(jnp.int32, (c, c), 1)
    tri = (row >= col).astype(F32)
    prow = lax.broadcasted_iota(jnp.int32, (c, gh * c), 0)
    plane = lax.broadcasted_iota(jnp.int32, (c, gh * c), 1)
    pcol = plane % c
    causal = prow >= pcol
    strict = prow > pcol
    diag = prow == pcol
    eye = diag.astype(F32)
    head_of_lane = plane // c
    low_half = lax.broadcasted_iota(jnp.int32, (c, LANE), 1) < c
    scale = HEAD ** -0.5
    zero_blk = jnp.zeros((c, HEAD), BF16)
    zero_state = jnp.zeros((HEAD, HEAD), BF16)
    nt = (((1,), (1,)), ((), ()))

    def block_diag(x):
        return jnp.concatenate([jnp.where(head_of_lane == h, x, jnp.zeros_like(x)) for h in range(gh)],
                               axis=0)

    def lane_blocks(block, h, zero):
        return jnp.concatenate([block if j == h else zero for j in range(gh)], axis=1)

    def pack(cols):
        return jnp.concatenate([jnp.where(low_half, cols[2 * i], cols[2 * i + 1])
                                for i in range(gh // 2)], axis=1)

    def decay_terms(r, rows):
        ba = ba_ref[r, rows, :]
        z = ba + hp_ref[1:2, :]
        softplus = jnp.maximum(z, 0.0) + jnp.log1p(jnp.exp(-jnp.abs(z)))
        la = -jnp.exp(hp_ref[0:1, :]) * softplus
        g_all = jnp.dot(tri, la, preferred_element_type=F32, precision=lax.Precision.HIGHEST)
        g_last = g_all[c - 1:c, :]
        bc = lambda x, lane: jnp.broadcast_to(x[:, lane:lane + 1], (c, LANE))
        eg_all = jnp.exp(g_all)
        ekd_all = jnp.exp(g_last - g_all)
        return dict(beta=[bc(jax.nn.sigmoid(ba), h) for h in heads],
                    g=[bc(g_all, n_heads + h) for h in heads],
                    eg=[bc(eg_all, n_heads + h) for h in heads],
                    ekd=[bc(ekd_all, n_heads + h) for h in heads],
                    egl=jnp.exp(g_last))

    def block_body(bi, carry):
        items = [(r, cc) for cc in range(nc) for r in range(nr)]
        rows_of = {cc: pl.ds(pl.multiple_of((bi * nc + cc) * c, c), c) for cc in range(nc)}
        streams = [(it, g) for it in items for g in groups]
        t = {it: decay_terms(it[0], rows_of[it[1]]) for it in items}
        kh = {(it, h): k_ref[it[0], rows_of[it[1]], hsl[h]] for it in items for h in heads}

        qkm, ms, ps = {}, {}, {}
        for st in streams:
            (r, cc), g = st
            hs = range(g * gh, (g + 1) * gh)
            gsl = slice(g * gh * HEAD, (g + 1) * gh * HEAD)
            lhs = jnp.concatenate([q_ref[r, rows_of[cc], gsl], k_ref[r, rows_of[cc], gsl]], axis=0)
            bd_k = jnp.concatenate([lane_blocks(kh[(r, cc), h], h - g * gh, zero_blk) for h in hs], axis=0)
            qk_kk = lax.dot_general(lhs, bd_k, nt, preferred_element_type=F32)
            gcol = pack([t[r, cc]["g"][h] for h in hs])
            grow = jnp.sum(jnp.where(diag, gcol, 0.0), axis=0, keepdims=True)
            decay = jnp.where(causal, jnp.exp(jnp.where(causal, gcol - grow, 0.0)), 0.0)
            qkm[st] = (qk_kk[:c] * (decay * scale)).astype(BF16)
            lmat = jnp.where(strict, pack([t[r, cc]["beta"][h] for h in hs]) * qk_kk[c:] * decay, 0.0)
            ms[st] = (-lmat).astype(BF16)
            ps[st] = eye - lmat
        ms = {st: _dot(ms[st], block_diag(ms[st])).astype(BF16) for st in streams}
        sq = c // 2
        while sq > 2:
            prod = {st: _dot(jnp.concatenate([ms[st], ps[st].astype(BF16)], axis=0), block_diag(ms[st]))
                    for st in streams}
            ms = {st: prod[st][:c].astype(BF16) for st in streams}
            ps = {st: ps[st] + prod[st][c:] for st in streams}
            sq //= 2
        ps = {st: ps[st] + _dot(ps[st].astype(BF16), block_diag(ms[st])) for st in streams}
        sols = {}
        for st in streams:
            (r, cc), g = st
            rhs = []
            for h in range(g * gh, (g + 1) * gh):
                beta, eg = t[r, cc]["beta"][h], t[r, cc]["eg"][h]
                vh = v_ref[r, rows_of[cc], hsl[h]].astype(F32)
                rhs.append(jnp.concatenate([vh * beta, kh[(r, cc), h].astype(F32) * (beta * eg)], axis=1))
            rhs = jnp.concatenate(rhs, axis=0).astype(BF16)
            sols[st] = _dot(block_diag(ps[st].astype(BF16)), rhs)

        def sol_rows(it, h):
            return sols[it, h // gh][(h % gh) * c:(h % gh + 1) * c]

        states = {(r, h): s_scr[r, h] for r in range(nr) for h in heads}
        for cc in range(nc):
            rows = rows_of[cc]
            ws = {}
            for r in range(nr):
                for a in range(n_heads // 2):
                    pair = (2 * a, 2 * a + 1)
                    w_row = jnp.concatenate([sol_rows((r, cc), h)[:, HEAD:] for h in pair], axis=1)
                    qd_row = jnp.concatenate(
                        [q_ref[r, rows, hsl[h]].astype(F32) * (t[r, cc]["eg"][h] * scale) for h in pair],
                        axis=1)
                    lhs = jnp.concatenate([w_row, qd_row], axis=0).astype(BF16)
                    s0, s1 = (states[r, h].astype(BF16) for h in pair)
                    rhs = jnp.concatenate([jnp.concatenate([s0, zero_state], axis=1),
                                           jnp.concatenate([zero_state, s1], axis=1)], axis=0)
                    ws[r, a] = _dot(lhs, rhs)

            def ws_of(r, h):
                return ws[r, h // 2][:, (h % 2) * HEAD:(h % 2 + 1) * HEAD]

            for r in range(nr):
                for g in groups:
                    hs = range(g * gh, (g + 1) * gh)
                    vn = [(sol_rows((r, cc), h)[:, :HEAD] - ws_of(r, h)[:c]).astype(BF16) for h in hs]
                    rhs = jnp.concatenate([lane_blocks(vn[i], i, zero_blk) for i in range(gh)], axis=0)
                    kd = [kh[(r, cc), h].astype(F32) * t[r, cc]["ekd"][h] for h in hs]
                    kdt = jnp.concatenate([jnp.concatenate([kd[2 * i], kd[2 * i + 1]], axis=0).T
                                           for i in range(gh // 2)], axis=1)
                    lhs = jnp.concatenate([qkm[(r, cc), g], kdt.astype(BF16)], axis=0)
                    out = _dot(lhs, rhs)
                    for i, h in enumerate(hs):
                        osl = slice(i * HEAD, (i + 1) * HEAD)
                        o = ws_of(r, h)[c:] + out[:c, osl]
                        gl = t[r, cc]["egl"][:, n_heads + h:n_heads + h + 1]
                        states[r, h] = states[r, h] * gl + out[c:, osl]
                        on = o * lax.rsqrt(jnp.mean(o * o, axis=-1, keepdims=True) + RMS_EPS) * onw_ref[...]
                        o_ref[r, rows, hsl[h]] = (on * gs_ref[r, rows, hsl[h]].astype(F32)).astype(BF16)
        for r in range(nr):
            for h in heads:
                s_scr[r, h] = states[r, h]
        return carry

    lax.fori_loop(0, ts // (c * nc), block_body, 0)


def _delta(q, k, v, ba, gs, hp, onw):
    b, s, d = q.shape
    assert HEAD == LANE and 2 * LANE % CHUNK == 0 and (d // HEAD) % (2 * LANE // CHUNK) == 0
    ts, nr = DELTA_TOKENS, DELTA_ROWS
    assert b % nr == 0 and s % ts == 0 and ts % (CHUNK * DELTA_CHUNKS) == 0
    tile = lambda w: pl.BlockSpec((nr, ts, w), lambda i, j: (i, j, 0))
    return pl.pallas_call(
        _delta_kernel,
        grid=(b // nr, s // ts),
        in_specs=[tile(d), tile(d), tile(d), tile(LANE), tile(d), _resident(hp.shape),
                  _resident(onw.shape)],
        out_specs=tile(d),
        out_shape=jax.ShapeDtypeStruct((b, s, d), BF16),
        scratch_shapes=[pltpu.VMEM((nr, d // HEAD, HEAD, HEAD), F32)],
        compiler_params=pltpu.CompilerParams(dimension_semantics=("arbitrary", "arbitrary"),
                                             vmem_limit_bytes=VMEM_LIMIT),
        name="delta",
    )(q, k, v, ba, gs, hp, onw)


def _mix_out_kernel(h_ref, n_ref, og_ref, wb_ref, lnw_ref, lnb_ref, ws_ref, bs_ref, wo_ref,
                    o_ref, gu_scr, vln_scr, sg_scr):
    ts, d = h_ref.shape
    n_groups = d // HEAD
    gc = GM_CHUNK
    n = n_ref[...]
    gu_scr[...] = _gelu(_dot(n, wb_ref[:, 0:d])).astype(BF16)
    gv = _gelu(_dot(n, wb_ref[:, d:2 * d]))
    mu = jnp.mean(gv, axis=-1, keepdims=True)
    xc = gv - mu
    ln = xc * lax.rsqrt(jnp.mean(xc * xc, axis=-1, keepdims=True) + LN_EPS)
    vln_scr[...] = (ln * lnw_ref[...] + lnb_ref[...]).astype(BF16)

    pos_r = lax.broadcasted_iota(jnp.int32, (gc, gc), 0) // CHUNK
    pos_c = lax.broadcasted_iota(jnp.int32, (gc, gc), 1) // CHUNK
    mask = pos_c <= pos_r
    for h in range(n_groups):
        hs = slice(h * HEAD, (h + 1) * HEAD)
        w_m = jnp.where(mask, ws_ref[h], 0.0).astype(BF16)
        bias = bs_ref[:, h:h + 1]
        for g in range(ts // gc):
            rows = slice(g * gc, (g + 1) * gc)
            mixed = _dot(w_m, vln_scr[rows, hs]) + bias
            sg_scr[rows, hs] = (gu_scr[rows, hs].astype(F32) * mixed).astype(BF16)
    y_a = _dot(og_ref[...], wo_ref[:, 0:d])
    y_b = _dot(sg_scr[...], wo_ref[:, d:2 * d])
    merged = (jax.nn.sigmoid(_dot(n, wb_ref[:, 2 * d:3 * d])) * y_a
              + jax.nn.sigmoid(_dot(n, wb_ref[:, 3 * d:4 * d])) * y_b)
    o_ref[...] = h_ref[...] + _dot(merged.astype(BF16), wo_ref[:, 2 * d:3 * d])


def _mix_out(h, n, og, wb, ln_w, ln_b, w_s, b_s_t, wo):
    b, s, d = h.shape
    ts = MIX_OUT_TOKENS
    tile = pl.BlockSpec((None, ts, d), lambda i, j: (i, j, 0))
    weights = (wb, ln_w, ln_b, w_s, b_s_t, wo)
    return pl.pallas_call(
        _mix_out_kernel,
        grid=(b, s // ts),
        in_specs=[tile] * 3 + [_resident(w.shape) for w in weights],
        out_specs=tile,
        out_shape=jax.ShapeDtypeStruct((b, s, d), F32),
        scratch_shapes=[pltpu.VMEM((ts, d), BF16)] * 3,
        compiler_params=pltpu.CompilerParams(dimension_semantics=("arbitrary", "arbitrary"),
                                             vmem_limit_bytes=VMEM_LIMIT),
        name="mix_out",
    )(h, n, og, *weights)


def _layer(h, p):
    b, s, d = h.shape
    n_heads = d // HEAD
    row = lambda a: a.reshape(1, -1).astype(F32)
    bf = lambda a: a.astype(BF16)

    h1, n = _ffn(h.reshape(b * s, d), row(p["ffn1_norm_w"]), bf(p["ffn1_w_gate"]), bf(p["ffn1_w_up"]),
                 bf(p["ffn1_w_down"]), row(p["mix_norm_w"]), False)
    h1, n = h1.reshape(b, s, d), n.reshape(b, s, d)

    w_in = p["w_in"]
    split = 4 * d + 2 * n_heads
    w_a = _pad_cols(bf(w_in[:, :split]), 4 * d + LANE)
    w_b = _pad_cols(bf(w_in[:, split:]), 4 * d)
    q, k, v, gs, ba = _mix_in(n, w_a, p["dn_conv_w"].astype(F32))

    hp = jnp.zeros((2, LANE), F32)
    hp = hp.at[0, n_heads:2 * n_heads].set(p["dn_a_log"].astype(F32))
    hp = hp.at[1, n_heads:2 * n_heads].set(p["dn_dt_bias"].astype(F32))
    og = _delta(q, k, v, ba, gs, hp, row(p["dn_out_norm_w"]))

    w_o = _pad_cols(jnp.concatenate([bf(p["dn_w_o"]), bf(p["gmlp_w_o"]), bf(p["w_out"])], axis=1), 3 * d)
    h2 = _mix_out(h1, n, og, w_b, row(p["sgu_norm_w"]), row(p["sgu_norm_b"]),
                  p["sgu_w_s"].astype(F32), jnp.swapaxes(p["sgu_b"], 0, 1).astype(F32), w_o)
    return h2


def kernel(x, ffn1_norm_w, ffn1_w_gate, ffn1_w_up, ffn1_w_down, mix_norm_w, w_in, dn_conv_w, dn_a_log, dn_dt_bias, dn_out_norm_w, dn_w_o, sgu_norm_w, sgu_norm_b, sgu_w_s, sgu_b, gmlp_w_o, w_out, ffn2_norm_w, ffn2_w_gate, ffn2_w_up, ffn2_w_down, final_norm_w):
    depth = ffn1_norm_w.shape[0]
    assert depth >= 1
    b, s, d = x.shape
    h = x
    for l in range(depth):
        p = dict(ffn1_norm_w=ffn1_norm_w[l], ffn1_w_gate=ffn1_w_gate[l], ffn1_w_up=ffn1_w_up[l],
                 ffn1_w_down=ffn1_w_down[l], mix_norm_w=mix_norm_w[l], w_in=w_in[l],
                 dn_conv_w=dn_conv_w[l], dn_a_log=dn_a_log[l], dn_dt_bias=dn_dt_bias[l],
                 dn_out_norm_w=dn_out_norm_w[l], dn_w_o=dn_w_o[l], sgu_norm_w=sgu_norm_w[l],
                 sgu_norm_b=sgu_norm_b[l], sgu_w_s=sgu_w_s[l], sgu_b=sgu_b[l], gmlp_w_o=gmlp_w_o[l],
                 w_out=w_out[l])
        h2 = _layer(h, p)
        last = l == depth - 1
        bf = lambda a: a.astype(BF16)
        out_w = final_norm_w.reshape(1, d) if last else jnp.ones((1, d), F32)
        h = _ffn(h2.reshape(b * s, d), ffn2_norm_w[l].reshape(1, d), bf(ffn2_w_gate[l]),
                 bf(ffn2_w_up[l]), bf(ffn2_w_down[l]), out_w, last)[0].reshape(b, s, d)
    return h
```

```python
import functools

import jax
import jax.numpy as jnp
from jax import lax
from jax.experimental import pallas as pl
from jax.experimental.pallas import tpu as pltpu

F32 = jnp.float32
BF16 = jnp.bfloat16

RMS_EPS = 1e-6
LN_EPS = 1e-5
L2_EPS = 1e-6
CHUNK = 64
GM_CHUNK = 128
HEAD = 128
CONV_K = 4
LANE = 128
CARRY = 8

FFN_TOKENS = 512
FUSED_TOKENS = 256
FFN_COLS = 256
MIX_OUT_TOKENS = 512
CONV_COLS = 256
DELTA_TOKENS = 512
DELTA_ROWS = 2
DELTA_CHUNKS = 4
VMEM_LIMIT = 56 * 1024 * 1024


def _resident(shape):
    nd = len(shape)
    return pl.BlockSpec(shape, lambda *_: (0,) * nd, pipeline_mode=pl.Buffered(1))


def _pad_cols(w, used):
    tiles = -(-used // LANE)
    tiles += 1 - tiles % 2
    return jnp.pad(w, ((0, 0), (0, tiles * LANE - w.shape[1])))


def _dot(a, b):
    return jnp.dot(a, b, preferred_element_type=F32)


def _rms(x, w):
    return x * lax.rsqrt(jnp.mean(x * x, axis=-1, keepdims=True) + RMS_EPS) * w


def _silu(x):
    return x * jax.nn.sigmoid(x)


def _gelu(x):
    return 0.5 * x * (1.0 + lax.erf(x * 0.7071067811865476))


def _swiglu_tasks(x_ref, nw_ref, wg_ref, wu_ref, wd_ref, n_scr, a_scr, finish):
    d_ff = wg_ref.shape[1]

    def norm():
        n_scr[...] = _rms(x_ref[...], nw_ref[...]).astype(BF16)

    def chunk(j):
        sl = slice(j * FFN_COLS, (j + 1) * FFN_COLS)
        n = n_scr[...]
        g = _dot(n, wg_ref[:, sl])
        u = _dot(n, wu_ref[:, sl])
        a_scr[:, sl] = (_silu(g) * u).astype(BF16)

    def down():
        finish(x_ref[...] + 0.5 * _dot(a_scr[...], wd_ref[:, 0:x_ref.shape[1]]))

    return [norm] + [functools.partial(chunk, j) for j in range(d_ff // FFN_COLS)] + [down]


def _mix_in_tasks(n_ref, w_ref, cw_ref, carry_scr, q_ref, k_ref, v_ref, gs_ref, ba_ref):
    ts, d = n_ref.shape
    assert CONV_K == 4 and cw_ref.shape[0] == CONV_K
    n_heads = d // HEAD
    proj = {}

    def project(part):
        proj[part] = _dot(n_ref[...], w_ref[:, part * d:(part + 1) * d])

    def conv(j):
        sl = slice(j * CONV_COLS, (j + 1) * CONV_COLS)
        off = (j * CONV_COLS) % d
        x = proj[j * CONV_COLS // d][:, off:off + CONV_COLS]
        w0, w1, w2, w3 = (cw_ref[kk:kk + 1, sl] for kk in range(CONV_K))
        xe = jnp.concatenate([carry_scr[:, sl], x], axis=0)
        carry_scr[:, sl] = x[ts - CARRY:, :]
        x2 = pltpu.roll(xe, 2, axis=0)
        acc = w3 * xe + w1 * x2 + pltpu.roll(w2 * xe + w0 * x2, 1, axis=0)
        y = _silu(acc[CARRY:, :])
        for i in range(CONV_COLS // HEAD):
            head = j * (CONV_COLS // HEAD) + i
            yh = y[:, i * HEAD:(i + 1) * HEAD]
            if head < 2 * n_heads:
                yh = yh * lax.rsqrt(jnp.sum(yh * yh, axis=-1, keepdims=True) + L2_EPS)
            dst = (q_ref, k_ref, v_ref)[head // n_heads]
            hs = slice((head % n_heads) * HEAD, (head % n_heads + 1) * HEAD)
            dst[:, hs] = yh.astype(BF16)

    def gates():
        n = n_ref[...]
        gs_ref[...] = _silu(_dot(n, w_ref[:, 3 * d:4 * d])).astype(BF16)
        ba_ref[...] = _dot(n, w_ref[:, 4 * d:4 * d + LANE])

    tasks = []
    for part in range(3):
        tasks.append(functools.partial(project, part))
        tasks += [functools.partial(conv, part * (d // CONV_COLS) + j) for j in range(d // CONV_COLS)]
    return tasks + [gates]


def _interleave(a, b):
    order = sorted([((i + 0.5) / len(a), 0, i) for i in range(len(a))]
                   + [((i + 0.5) / len(b), 1, i) for i in range(len(b))])
    for _, which, i in order:
        (a, b)[which][i]()


def _ffn_mix_kernel(x_ref, nw_ref, wg_ref, wu_ref, wd_ref, ow_ref, wa_ref, cw_ref,
                    h_ref, nn_ref, q_ref, k_ref, v_ref, gs_ref, ba_ref,
                    n_scr, a_scr, prev_scr, carry_scr, *, tiles_per_seq):
    i = pl.program_id(0)

    @pl.when(i == 0)
    def _():
        prev_scr[...] = jnp.zeros(prev_scr.shape, BF16)

    @pl.when(lax.rem(i + tiles_per_seq - 1, tiles_per_seq) == 0)
    def _():
        carry_scr[...] = jnp.zeros(carry_scr.shape, F32)

    slot = lax.rem(i, 2)

    def finish(h):
        normed = _rms(h, ow_ref[...]).astype(BF16)
        h_ref[...] = h
        nn_ref[...] = normed
        prev_scr[slot] = normed

    _interleave(_swiglu_tasks(x_ref, nw_ref, wg_ref, wu_ref, wd_ref, n_scr, a_scr, finish),
                _mix_in_tasks(prev_scr.at[1 - slot], wa_ref, cw_ref, carry_scr,
                              q_ref, k_ref, v_ref, gs_ref, ba_ref))


def _ffn_mix(x2d, seq_len, norm_w, wg, wu, wd, out_norm_w, w_a, conv_w):
    t, d = x2d.shape
    tm = FUSED_TOKENS
    assert seq_len % tm == 0 and t % seq_len == 0
    nt = t // tm
    wd = _pad_cols(wd, d)
    cur = lambda w: pl.BlockSpec((tm, w), lambda i: (jnp.minimum(i, nt - 1), 0))
    lag = lambda w: pl.BlockSpec((tm, w), lambda i: (jnp.maximum(i - 1, 0), 0))
    bf = lambda w: jax.ShapeDtypeStruct((t, w), BF16)
    weights = (norm_w, wg, wu, wd, out_norm_w, w_a, conv_w)
    return pl.pallas_call(
        functools.partial(_ffn_mix_kernel, tiles_per_seq=seq_len // tm),
        grid=(nt + 1,),
        in_specs=[cur(d)] + [_resident(w.shape) for w in weights],
        out_specs=[cur(d), cur(d), lag(d), lag(d), lag(d), lag(d), lag(LANE)],
        out_shape=[jax.ShapeDtypeStruct((t, d), F32), bf(d), bf(d), bf(d), bf(d), bf(d),
                   jax.ShapeDtypeStruct((t, LANE), F32)],
        scratch_shapes=[pltpu.VMEM((tm, d), BF16), pltpu.VMEM((tm, wg.shape[1]), BF16),
                        pltpu.VMEM((2, tm, d), BF16), pltpu.VMEM((CARRY, 3 * d), F32)],
        compiler_params=pltpu.CompilerParams(dimension_semantics=("arbitrary",),
                                             vmem_limit_bytes=VMEM_LIMIT),
        name="ffn_mix_in",
    )(x2d, *weights)


def _ffn_final_kernel(x_ref, nw_ref, wg_ref, wu_ref, wd_ref, ow_ref, o_ref, n_scr, a_scr):
    def finish(h):
        o_ref[...] = _rms(h, ow_ref[...])

    for task in _swiglu_tasks(x_ref, nw_ref, wg_ref, wu_ref, wd_ref, n_scr, a_scr, finish):
        task()


def _ffn_final(x2d, norm_w, wg, wu, wd, out_norm_w):
    t, d = x2d.shape
    tm = FFN_TOKENS
    wd = _pad_cols(wd, d)
    row = pl.BlockSpec((tm, d), lambda i: (i, 0))
    weights = (norm_w, wg, wu, wd, out_norm_w)
    return pl.pallas_call(
        _ffn_final_kernel,
        grid=(t // tm,),
        in_specs=[row] + [_resident(w.shape) for w in weights],
        out_specs=row,
        out_shape=jax.ShapeDtypeStruct((t, d), F32),
        scratch_shapes=[pltpu.VMEM((tm, d), BF16), pltpu.VMEM((tm, wg.shape[1]), BF16)],
        compiler_params=pltpu.CompilerParams(dimension_semantics=("arbitrary",),
                                             vmem_limit_bytes=VMEM_LIMIT),
        name="ffn_final",
    )(x2d, *weights)


def _delta_kernel(q_ref, k_ref, v_ref, ba_ref, gs_ref, hp_ref, onw_ref, o_ref, s_scr):
    nr, ts, d = q_ref.shape
    n_heads = d // HEAD
    c = CHUNK
    nc = DELTA_CHUNKS
    gh = 2 * LANE // c
    n_groups = n_heads // gh
    heads = range(n_heads)
    groups = range(n_groups)
    hsl = [slice(h * HEAD, (h + 1) * HEAD) for h in heads]

    @pl.when(pl.program_id(1) == 0)
    def _():
        s_scr[...] = jnp.zeros(s_scr.shape, F32)

    row = lax.broadcasted_iota(jnp.int32, (c, c), 0)
    col = lax.broadcasted_iota(jnp.int32, (c, c), 1)
    tri = (row >= col).astype(F32)
    prow = lax.broadcasted_iota(jnp.int32, (c, gh * c), 0)
    plane = lax.broadcasted_iota(jnp.int32, (c, gh * c), 1)
    pcol = plane % c
    causal = prow >= pcol
    strict = prow > pcol
    diag = prow == pcol
    eye = diag.astype(F32)
    head_of_lane = plane // c
    low_half = lax.broadcasted_iota(jnp.int32, (c, LANE), 1) < c
    scale = HEAD ** -0.5
    zero_blk = jnp.zeros((c, HEAD), BF16)
    zero_state = jnp.zeros((HEAD, HEAD), BF16)
    nt = (((1,), (1,)), ((), ()))

    def block_diag(x):
        return jnp.concatenate([jnp.where(head_of_lane == h, x, jnp.zeros_like(x)) for h in range(gh)],
                               axis=0)

    def lane_blocks(block, h, zero):
        return jnp.concatenate([block if j == h else zero for j in range(gh)], axis=1)

    def pack(cols):
        return jnp.concatenate([jnp.where(low_half, cols[2 * i], cols[2 * i + 1])
                                for i in range(gh // 2)], axis=1)

    def decay_terms(r, rows):
        ba = ba_ref[r, rows, :]
        z = ba + hp_ref[1:2, :]
        softplus = jnp.maximum(z, 0.0) + jnp.log1p(jnp.exp(-jnp.abs(z)))
        la = -jnp.exp(hp_ref[0:1, :]) * softplus
        g_all = jnp.dot(tri, la, preferred_element_type=F32, precision=lax.Precision.HIGHEST)
        g_last = g_all[c - 1:c, :]
        bc = lambda x, lane: jnp.broadcast_to(x[:, lane:lane + 1], (c, LANE))
        eg_all = jnp.exp(g_all)
        ekd_all = jnp.exp(g_last - g_all)
        return dict(beta=[bc(jax.nn.sigmoid(ba), h) for h in heads],
                    g=[bc(g_all, n_heads + h) for h in heads],
                    eg=[bc(eg_all, n_heads + h) for h in heads],
                    ekd=[bc(ekd_all, n_heads + h) for h in heads],
                    egl=jnp.exp(g_last))

    def block_body(bi, carry):
        items = [(r, cc) for cc in range(nc) for r in range(nr)]
        rows_of = {cc: pl.ds(pl.multiple_of((bi * nc + cc) * c, c), c) for cc in range(nc)}
        streams = [(it, g) for it in items for g in groups]
        t = {it: decay_terms(it[0], rows_of[it[1]]) for it in items}
        kh = {(it, h): k_ref[it[0], rows_of[it[1]], hsl[h]] for it in items for h in heads}

        qkm, ms, ps = {}, {}, {}
        for st in streams:
            (r, cc), g = st
            hs = range(g * gh, (g + 1) * gh)
            gsl = slice(g * gh * HEAD, (g + 1) * gh * HEAD)
            lhs = jnp.concatenate([q_ref[r, rows_of[cc], gsl], k_ref[r, rows_of[cc], gsl]], axis=0)
            bd_k = jnp.concatenate([lane_blocks(kh[(r, cc), h], h - g * gh, zero_blk) for h in hs], axis=0)
            qk_kk = lax.dot_general(lhs, bd_k, nt, preferred_element_type=F32)
            gcol = pack([t[r, cc]["g"][h] for h in hs])
            grow = jnp.sum(jnp.where(diag, gcol, 0.0), axis=0, keepdims=True)
            decay = jnp.where(causal, jnp.exp(jnp.where(causal, gcol - grow, 0.0)), 0.0)
            qkm[st] = (qk_kk[:c] * (decay * scale)).astype(BF16)
            lmat = jnp.where(strict, pack([t[r, cc]["beta"][h] for h in hs]) * qk_kk[c:] * decay, 0.0)
            ms[st] = (-lmat).astype(BF16)
            ps[st] = eye - lmat
        ms = {st: _dot(ms[st], block_diag(ms[st])).astype(BF16) for st in streams}
        sq = c // 2
        while sq > 2:
            prod = {st: _dot(jnp.concatenate([ms[st], ps[st].astype(BF16)], axis=0), block_diag(ms[st]))
                    for st in streams}
            ms = {st: prod[st][:c].astype(BF16) for st in streams}
            ps = {st: ps[st] + prod[st][c:] for st in streams}
            sq //= 2
        ps = {st: ps[st] + _dot(ps[st].astype(BF16), block_diag(ms[st])) for st in streams}
        sols = {}
        for st in streams:
            (r, cc), g = st
            rhs = []
            for h in range(g * gh, (g + 1) * gh):
                beta, eg = t[r, cc]["beta"][h], t[r, cc]["eg"][h]
                vh = v_ref[r, rows_of[cc], hsl[h]].astype(F32)
                rhs.append(jnp.concatenate([vh * beta, kh[(r, cc), h].astype(F32) * (beta * eg)], axis=1))
            rhs = jnp.concatenate(rhs, axis=0).astype(BF16)
            sols[st] = _dot(block_diag(ps[st].astype(BF16)), rhs)

        def sol_rows(it, h):
            return sols[it, h // gh][(h % gh) * c:(h % gh + 1) * c]

        states = {(r, h): s_scr[r, h] for r in range(nr) for h in heads}
        for cc in range(nc):
            rows = rows_of[cc]
            ws = {}
            for r in range(nr):
                for a in range(n_heads // 2):
                    pair = (2 * a, 2 * a + 1)
                    w_row = jnp.concatenate([sol_rows((r, cc), h)[:, HEAD:] for h in pair], axis=1)
                    qd_row = jnp.concatenate(
                        [q_ref[r, rows, hsl[h]].astype(F32) * (t[r, cc]["eg"][h] * scale) for h in pair],
                        axis=1)
                    lhs = jnp.concatenate([w_row, qd_row], axis=0).astype(BF16)
                    s0, s1 = (states[r, h].astype(BF16) for h in pair)
                    rhs = jnp.concatenate([jnp.concatenate([s0, zero_state], axis=1),
                                           jnp.concatenate([zero_state, s1], axis=1)], axis=0)
                    ws[r, a] = _dot(lhs, rhs)

            def ws_of(r, h):
                return ws[r, h // 2][:, (h % 2) * HEAD:(h % 2 + 1) * HEAD]

            for r in range(nr):
                for g in groups:
                    hs = range(g * gh, (g + 1) * gh)
                    vn = [(sol_rows((r, cc), h)[:, :HEAD] - ws_of(r, h)[:c]).astype(BF16) for h in hs]
                    rhs = jnp.concatenate([lane_blocks(vn[i], i, zero_blk) for i in range(gh)], axis=0)
                    kd = [kh[(r, cc), h].astype(F32) * t[r, cc]["ekd"][h] for h in hs]
                    kdt = jnp.concatenate([jnp.concatenate([kd[2 * i], kd[2 * i + 1]], axis=0).T
                                           for i in range(gh // 2)], axis=1)
                    lhs = jnp.concatenate([qkm[(r, cc), g], kdt.astype(BF16)], axis=0)
                    out = _dot(lhs, rhs)
                    for i, h in enumerate(hs):
                        osl = slice(i * HEAD, (i + 1) * HEAD)
                        o = ws_of(r, h)[c:] + out[:c, osl]
                        gl = t[r, cc]["egl"][:, n_heads + h:n_heads + h + 1]
                        states[r, h] = states[r, h] * gl + out[c:, osl]
                        on = o * lax.rsqrt(jnp.mean(o * o, axis=-1, keepdims=True) + RMS_EPS) * onw_ref[...]
                        o_ref[r, rows, hsl[h]] = (on * gs_ref[r, rows, hsl[h]].astype(F32)).astype(BF16)
        for r in range(nr):
            for h in heads:
                s_scr[r, h] = states[r, h]
        return carry

    lax.fori_loop(0, ts // (c * nc), block_body, 0)


def _delta(q, k, v, ba, gs, hp, onw):
    b, s, d = q.shape
    assert HEAD == LANE and 2 * LANE % CHUNK == 0 and (d // HEAD) % (2 * LANE // CHUNK) == 0
    ts, nr = DELTA_TOKENS, DELTA_ROWS
    assert b % nr == 0 and s % ts == 0 and ts % (CHUNK * DELTA_CHUNKS) == 0
    tile = lambda w: pl.BlockSpec((nr, ts, w), lambda i, j: (i, j, 0))
    return pl.pallas_call(
        _delta_kernel,
        grid=(b // nr, s // ts),
        in_specs=[tile(d), tile(d), tile(d), tile(LANE), tile(d), _resident(hp.shape),
                  _resident(onw.shape)],
        out_specs=tile(d),
        out_shape=jax.ShapeDtypeStruct((b, s, d), BF16),
        scratch_shapes=[pltpu.VMEM((nr, d // HEAD, HEAD, HEAD), F32)],
        compiler_params=pltpu.CompilerParams(dimension_semantics=("arbitrary", "arbitrary"),
                                             vmem_limit_bytes=VMEM_LIMIT),
        name="delta",
    )(q, k, v, ba, gs, hp, onw)


def _mix_out_kernel(h_ref, n_ref, og_ref, wb_ref, lnw_ref, lnb_ref, ws_ref, bs_ref, wo_ref,
                    o_ref, gu_scr, vln_scr, sg_scr):
    ts, d = h_ref.shape
    n_groups = d // HEAD
    gc = GM_CHUNK
    n = n_ref[...]
    gu_scr[...] = _gelu(_dot(n, wb_ref[:, 0:d])).astype(BF16)
    gv = _gelu(_dot(n, wb_ref[:, d:2 * d]))
    mu = jnp.mean(gv, axis=-1, keepdims=True)
    xc = gv - mu
    ln = xc * lax.rsqrt(jnp.mean(xc * xc, axis=-1, keepdims=True) + LN_EPS)
    vln_scr[...] = (ln * lnw_ref[...] + lnb_ref[...]).astype(BF16)

    pos_r = lax.broadcasted_iota(jnp.int32, (gc, gc), 0) // CHUNK
    pos_c = lax.broadcasted_iota(jnp.int32, (gc, gc), 1) // CHUNK
    mask = pos_c <= pos_r
    for h in range(n_groups):
        hs = slice(h * HEAD, (h + 1) * HEAD)
        w_m = jnp.where(mask, ws_ref[h], 0.0).astype(BF16)
        bias = bs_ref[:, h:h + 1]
        for g in range(ts // gc):
            rows = slice(g * gc, (g + 1) * gc)
            mixed = _dot(w_m, vln_scr[rows, hs]) + bias
            sg_scr[rows, hs] = (gu_scr[rows, hs].astype(F32) * mixed).astype(BF16)
    y_a = _dot(og_ref[...], wo_ref[:, 0:d])
    y_b = _dot(sg_scr[...], wo_ref[:, d:2 * d])
    merged = (jax.nn.sigmoid(_dot(n, wb_ref[:, 2 * d:3 * d])) * y_a
              + jax.nn.sigmoid(_dot(n, wb_ref[:, 3 * d:4 * d])) * y_b)
    o_ref[...] = h_ref[...] + _dot(merged.astype(BF16), wo_ref[:, 2 * d:3 * d])


def _mix_out(h, n, og, wb, ln_w, ln_b, w_s, b_s_t, wo):
    b, s, d = h.shape
    ts = MIX_OUT_TOKENS
    tile = pl.BlockSpec((None, ts, d), lambda i, j: (i, j, 0))
    weights = (wb, ln_w, ln_b, w_s, b_s_t, wo)
    return pl.pallas_call(
        _mix_out_kernel,
        grid=(b, s // ts),
        in_specs=[tile] * 3 + [_resident(w.shape) for w in weights],
        out_specs=tile,
        out_shape=jax.ShapeDtypeStruct((b, s, d), F32),
        scratch_shapes=[pltpu.VMEM((ts, d), BF16)] * 3,
        compiler_params=pltpu.CompilerParams(dimension_semantics=("arbitrary", "arbitrary"),
                                             vmem_limit_bytes=VMEM_LIMIT),
        name="mix_out",
    )(h, n, og, *weights)


def _layer(h, p):
    b, s, d = h.shape
    n_heads = d // HEAD
    row = lambda a: a.reshape(1, -1).astype(F32)
    bf = lambda a: a.astype(BF16)

    w_in = p["w_in"]
    split = 4 * d + 2 * n_heads
    w_a = _pad_cols(bf(w_in[:, :split]), 4 * d + LANE)
    w_b = _pad_cols(bf(w_in[:, split:]), 4 * d)
    outs = _ffn_mix(h.reshape(b * s, d), s, row(p["ffn1_norm_w"]), bf(p["ffn1_w_gate"]),
                    bf(p["ffn1_w_up"]), bf(p["ffn1_w_down"]), row(p["mix_norm_w"]), w_a,
                    p["dn_conv_w"].astype(F32))
    h1, n, q, k, v, gs, ba = (o.reshape(b, s, -1) for o in outs)

    hp = jnp.zeros((2, LANE), F32)
    hp = hp.at[0, n_heads:2 * n_heads].set(p["dn_a_log"].astype(F32))
    hp = hp.at[1, n_heads:2 * n_heads].set(p["dn_dt_bias"].astype(F32))
    og = _delta(q, k, v, ba, gs, hp, row(p["dn_out_norm_w"]))

    w_o = _pad_cols(jnp.concatenate([bf(p["dn_w_o"]), bf(p["gmlp_w_o"]), bf(p["w_out"])], axis=1), 3 * d)
    h2 = _mix_out(h1, n, og, w_b, row(p["sgu_norm_w"]), row(p["sgu_norm_b"]),
                  p["sgu_w_s"].astype(F32), jnp.swapaxes(p["sgu_b"], 0, 1).astype(F32), w_o)
    return h2


def kernel(x, ffn1_norm_w, ffn1_w_gate, ffn1_w_up, ffn1_w_down, mix_norm_w, w_in, dn_conv_w, dn_a_log, dn_dt_bias, dn_out_norm_w, dn_w_o, sgu_norm_w, sgu_norm_b, sgu_w_s, sgu_b, gmlp_w_o, w_out, ffn2_norm_w, ffn2_w_gate, ffn2_w_up, ffn2_w_down, final_norm_w):
    assert ffn1_norm_w.shape[0] == 1, "the call pipeline is wired for the fixed depth of one layer"
    b, s, d = x.shape
    l = 0
    p = dict(ffn1_norm_w=ffn1_norm_w[l], ffn1_w_gate=ffn1_w_gate[l], ffn1_w_up=ffn1_w_up[l],
             ffn1_w_down=ffn1_w_down[l], mix_norm_w=mix_norm_w[l], w_in=w_in[l],
             dn_conv_w=dn_conv_w[l], dn_a_log=dn_a_log[l], dn_dt_bias=dn_dt_bias[l],
             dn_out_norm_w=dn_out_norm_w[l], dn_w_o=dn_w_o[l], sgu_norm_w=sgu_norm_w[l],
             sgu_norm_b=sgu_norm_b[l], sgu_w_s=sgu_w_s[l], sgu_b=sgu_b[l], gmlp_w_o=gmlp_w_o[l],
             w_out=w_out[l])
    h2 = _layer(x, p)
    bf = lambda a: a.astype(BF16)
    out = _ffn_final(h2.reshape(b * s, d), ffn2_norm_w[l].reshape(1, d), bf(ffn2_w_gate[l]),
                     bf(ffn2_w_up[l]), bf(ffn2_w_down[l]), final_norm_w.reshape(1, d))
    return out.reshape(b, s, d)
```

```python
import functools

import jax
import jax.numpy as jnp
from jax import lax
from jax.experimental import pallas as pl
from jax.experimental.pallas import tpu as pltpu

F32 = jnp.float32
BF16 = jnp.bfloat16

RMS_EPS = 1e-6
LN_EPS = 1e-5
L2_EPS = 1e-6
CHUNK = 64
GM_CHUNK = 128
HEAD = 128
CONV_K = 4
LANE = 128
CARRY = 8

FFN_TOKENS = 512
FUSED_TOKENS = 256
FFN_COLS = 256
CONV_COLS = 256
DELTA_TOKENS = 256
DELTA_ROWS = 2
VMEM_LIMIT = 56 * 1024 * 1024


def _resident(shape):
    nd = len(shape)
    return pl.BlockSpec(shape, lambda *_: (0,) * nd, pipeline_mode=pl.Buffered(1))


def _pad_cols(w, used):
    tiles = -(-used // LANE)
    tiles += 1 - tiles % 2
    return jnp.pad(w, ((0, 0), (0, tiles * LANE - w.shape[1])))


def _dot(a, b):
    return jnp.dot(a, b, preferred_element_type=F32)


def _rms(x, w):
    return x * lax.rsqrt(jnp.mean(x * x, axis=-1, keepdims=True) + RMS_EPS) * w


def _silu(x):
    return x * jax.nn.sigmoid(x)


def _gelu(x):
    return 0.5 * x * (1.0 + lax.erf(x * 0.7071067811865476))


def _swiglu_tasks(x_ref, nw_ref, wg_ref, wu_ref, wd_ref, n_scr, a_scr, finish):
    d_ff = wg_ref.shape[1]

    def norm():
        n_scr[...] = _rms(x_ref[...], nw_ref[...]).astype(BF16)

    def chunk(j):
        sl = slice(j * FFN_COLS, (j + 1) * FFN_COLS)
        n = n_scr[...]
        g = _dot(n, wg_ref[:, sl])
        u = _dot(n, wu_ref[:, sl])
        a_scr[:, sl] = (_silu(g) * u).astype(BF16)

    def down():
        finish(x_ref[...] + 0.5 * _dot(a_scr[...], wd_ref[:, 0:x_ref.shape[1]]))

    return [norm] + [functools.partial(chunk, j) for j in range(d_ff // FFN_COLS)] + [down]


def _mix_in_tasks(n_ref, w_ref, cw_ref, carry_scr, q_ref, k_ref, v_ref, gs_ref, ba_ref):
    ts, d = n_ref.shape
    assert CONV_K == 4 and cw_ref.shape[0] == CONV_K
    n_heads = d // HEAD
    proj = {}

    def project(part):
        proj[part] = _dot(n_ref[...], w_ref[:, part * d:(part + 1) * d])

    def conv(j):
        sl = slice(j * CONV_COLS, (j + 1) * CONV_COLS)
        off = (j * CONV_COLS) % d
        x = proj[j * CONV_COLS // d][:, off:off + CONV_COLS]
        w0, w1, w2, w3 = (cw_ref[kk:kk + 1, sl] for kk in range(CONV_K))
        xe = jnp.concatenate([carry_scr[:, sl], x], axis=0)
        carry_scr[:, sl] = x[ts - CARRY:, :]
        x2 = pltpu.roll(xe, 2, axis=0)
        acc = w3 * xe + w1 * x2 + pltpu.roll(w2 * xe + w0 * x2, 1, axis=0)
        y = _silu(acc[CARRY:, :])
        for i in range(CONV_COLS // HEAD):
            head = j * (CONV_COLS // HEAD) + i
            yh = y[:, i * HEAD:(i + 1) * HEAD]
            if head < 2 * n_heads:
                yh = yh * lax.rsqrt(jnp.sum(yh * yh, axis=-1, keepdims=True) + L2_EPS)
            dst = (q_ref, k_ref, v_ref)[head // n_heads]
            hs = slice((head % n_heads) * HEAD, (head % n_heads + 1) * HEAD)
            dst[:, hs] = yh.astype(BF16)

    def gates():
        n = n_ref[...]
        gs_ref[...] = _silu(_dot(n, w_ref[:, 3 * d:4 * d])).astype(BF16)
        ba_ref[...] = _dot(n, w_ref[:, 4 * d:4 * d + LANE])

    tasks = []
    for part in range(3):
        tasks.append(functools.partial(project, part))
        tasks += [functools.partial(conv, part * (d // CONV_COLS) + j) for j in range(d // CONV_COLS)]
    return tasks + [gates]


def _interleave(a, b):
    order = sorted([((i + 0.5) / len(a), 0, i) for i in range(len(a))]
                   + [((i + 0.5) / len(b), 1, i) for i in range(len(b))])
    for _, which, i in order:
        (a, b)[which][i]()


def _ffn_mix_kernel(x_ref, nw_ref, wg_ref, wu_ref, wd_ref, ow_ref, wa_ref, cw_ref,
                    h_ref, nn_ref, q_ref, k_ref, v_ref, gs_ref, ba_ref,
                    n_scr, a_scr, prev_scr, carry_scr, *, tiles_per_seq):
    i = pl.program_id(0)

    @pl.when(i == 0)
    def _():
        prev_scr[...] = jnp.zeros(prev_scr.shape, BF16)

    @pl.when(lax.rem(i + tiles_per_seq - 1, tiles_per_seq) == 0)
    def _():
        carry_scr[...] = jnp.zeros(carry_scr.shape, F32)

    slot = lax.rem(i, 2)

    def finish(h):
        normed = _rms(h, ow_ref[...]).astype(BF16)
        h_ref[...] = h
        nn_ref[...] = normed
        prev_scr[slot] = normed

    _interleave(_swiglu_tasks(x_ref, nw_ref, wg_ref, wu_ref, wd_ref, n_scr, a_scr, finish),
                _mix_in_tasks(prev_scr.at[1 - slot], wa_ref, cw_ref, carry_scr,
                              q_ref, k_ref, v_ref, gs_ref, ba_ref))


def _ffn_mix(x2d, seq_len, norm_w, wg, wu, wd, out_norm_w, w_a, conv_w):
    t, d = x2d.shape
    tm = FUSED_TOKENS
    assert seq_len % tm == 0 and t % seq_len == 0
    nt = t // tm
    wd = _pad_cols(wd, d)
    cur = lambda w: pl.BlockSpec((tm, w), lambda i: (jnp.minimum(i, nt - 1), 0))
    lag = lambda w: pl.BlockSpec((tm, w), lambda i: (jnp.maximum(i - 1, 0), 0))
    bf = lambda w: jax.ShapeDtypeStruct((t, w), BF16)
    weights = (norm_w, wg, wu, wd, out_norm_w, w_a, conv_w)
    return pl.pallas_call(
        functools.partial(_ffn_mix_kernel, tiles_per_seq=seq_len // tm),
        grid=(nt + 1,),
        in_specs=[cur(d)] + [_resident(w.shape) for w in weights],
        out_specs=[cur(d), cur(d), lag(d), lag(d), lag(d), lag(d), lag(LANE)],
        out_shape=[jax.ShapeDtypeStruct((t, d), F32), bf(d), bf(d), bf(d), bf(d), bf(d),
                   jax.ShapeDtypeStruct((t, LANE), F32)],
        scratch_shapes=[pltpu.VMEM((tm, d), BF16), pltpu.VMEM((tm, wg.shape[1]), BF16),
                        pltpu.VMEM((2, tm, d), BF16), pltpu.VMEM((CARRY, 3 * d), F32)],
        compiler_params=pltpu.CompilerParams(dimension_semantics=("arbitrary",),
                                             vmem_limit_bytes=VMEM_LIMIT),
        name="ffn_mix_in",
    )(x2d, *weights)


def _ffn_final_kernel(x_ref, nw_ref, wg_ref, wu_ref, wd_ref, ow_ref, o_ref, n_scr, a_scr):
    def finish(h):
        o_ref[...] = _rms(h, ow_ref[...])

    for task in _swiglu_tasks(x_ref, nw_ref, wg_ref, wu_ref, wd_ref, n_scr, a_scr, finish):
        task()


def _ffn_final(x2d, norm_w, wg, wu, wd, out_norm_w):
    t, d = x2d.shape
    tm = FFN_TOKENS
    wd = _pad_cols(wd, d)
    row = pl.BlockSpec((tm, d), lambda i: (i, 0))
    weights = (norm_w, wg, wu, wd, out_norm_w)
    return pl.pallas_call(
        _ffn_final_kernel,
        grid=(t // tm,),
        in_specs=[row] + [_resident(w.shape) for w in weights],
        out_specs=row,
        out_shape=jax.ShapeDtypeStruct((t, d), F32),
        scratch_shapes=[pltpu.VMEM((tm, d), BF16), pltpu.VMEM((tm, wg.shape[1]), BF16)],
        compiler_params=pltpu.CompilerParams(dimension_semantics=("arbitrary",),
                                             vmem_limit_bytes=VMEM_LIMIT),
        name="ffn_final",
    )(x2d, *weights)


def _delta_tasks(q_ref, k_ref, v_ref, ba_ref, gs_ref, hp_ref, onw_ref, s_scr, store_out):
    nr, ts, d = q_ref.shape
    n_heads = d // HEAD
    c = CHUNK
    nc = ts // c
    gh = 2 * LANE // c
    n_groups = n_heads // gh
    heads = range(n_heads)
    groups = range(n_groups)
    hsl = [slice(h * HEAD, (h + 1) * HEAD) for h in heads]

    row = lax.broadcasted_iota(jnp.int32, (c, c), 0)
    col = lax.broadcasted_iota(jnp.int32, (c, c), 1)
    tri = (row >= col).astype(F32)
    prow = lax.broadcasted_iota(jnp.int32, (c, gh * c), 0)
    plane = lax.broadcasted_iota(jnp.int32, (c, gh * c), 1)
    pcol = plane % c
    causal = prow >= pcol
    strict = prow > pcol
    diag = prow == pcol
    eye = diag.astype(F32)
    head_of_lane = plane // c
    low_half = lax.broadcasted_iota(jnp.int32, (c, LANE), 1) < c
    scale = HEAD ** -0.5
    zero_blk = jnp.zeros((c, HEAD), BF16)
    zero_state = jnp.zeros((HEAD, HEAD), BF16)
    nt = (((1,), (1,)), ((), ()))

    def block_diag(x):
        return jnp.concatenate([jnp.where(head_of_lane == h, x, jnp.zeros_like(x)) for h in range(gh)],
                               axis=0)

    def lane_blocks(block, h, zero):
        return jnp.concatenate([block if j == h else zero for j in range(gh)], axis=1)

    def pack(cols):
        return jnp.concatenate([jnp.where(low_half, cols[2 * i], cols[2 * i + 1])
                                for i in range(gh // 2)], axis=1)

    def decay_terms(r, rows):
        ba = ba_ref[r, rows, :]
        z = ba + hp_ref[1:2, :]
        softplus = jnp.maximum(z, 0.0) + jnp.log1p(jnp.exp(-jnp.abs(z)))
        la = -jnp.exp(hp_ref[0:1, :]) * softplus
        g_all = jnp.dot(tri, la, preferred_element_type=F32, precision=lax.Precision.HIGHEST)
        g_last = g_all[c - 1:c, :]
        bc = lambda x, lane: jnp.broadcast_to(x[:, lane:lane + 1], (c, LANE))
        eg_all = jnp.exp(g_all)
        ekd_all = jnp.exp(g_last - g_all)
        return dict(beta=[bc(jax.nn.sigmoid(ba), h) for h in heads],
                    g=[bc(g_all, n_heads + h) for h in heads],
                    eg=[bc(eg_all, n_heads + h) for h in heads],
                    ekd=[bc(ekd_all, n_heads + h) for h in heads],
                    egl=jnp.exp(g_last))

    items = [(r, cc) for cc in range(nc) for r in range(nr)]
    rows_of = {cc: slice(cc * c, (cc + 1) * c) for cc in range(nc)}
    streams = [(it, g) for it in items for g in groups]
    t, kh, qkm, ms, ps, sols, states = {}, {}, {}, {}, {}, {}, {}

    def prepare():
        for it in items:
            t[it] = decay_terms(it[0], rows_of[it[1]])
            for h in heads:
                kh[it, h] = k_ref[it[0], rows_of[it[1]], hsl[h]]
        for r in range(nr):
            for h in heads:
                states[r, h] = s_scr[r, h]

    def scores(part):
        for st in part:
            (r, cc), g = st
            hs = range(g * gh, (g + 1) * gh)
            gsl = slice(g * gh * HEAD, (g + 1) * gh * HEAD)
            lhs = jnp.concatenate([q_ref[r, rows_of[cc], gsl], k_ref[r, rows_of[cc], gsl]], axis=0)
            bd_k = jnp.concatenate([lane_blocks(kh[(r, cc), h], h - g * gh, zero_blk) for h in hs], axis=0)
            qk_kk = lax.dot_general(lhs, bd_k, nt, preferred_element_type=F32)
            gcol = pack([t[r, cc]["g"][h] for h in hs])
            grow = jnp.sum(jnp.where(diag, gcol, 0.0), axis=0, keepdims=True)
            decay = jnp.where(causal, jnp.exp(jnp.where(causal, gcol - grow, 0.0)), 0.0)
            qkm[st] = (qk_kk[:c] * (decay * scale)).astype(BF16)
            lmat = jnp.where(strict, pack([t[r, cc]["beta"][h] for h in hs]) * qk_kk[c:] * decay, 0.0)
            ms[st] = (-lmat).astype(BF16)
            ps[st] = eye - lmat
    def square():
        for st in streams:
            ms[st] = _dot(ms[st], block_diag(ms[st])).astype(BF16)

    def square_and_extend():
        for st in streams:
            prod = _dot(jnp.concatenate([ms[st], ps[st].astype(BF16)], axis=0), block_diag(ms[st]))
            ms[st] = prod[:c].astype(BF16)
            ps[st] = ps[st] + prod[c:]

    def extend():
        for st in streams:
            ps[st] = ps[st] + _dot(ps[st].astype(BF16), block_diag(ms[st]))

    def solve(part):
        for st in part:
            (r, cc), g = st
            rhs = []
            for h in range(g * gh, (g + 1) * gh):
                beta, eg = t[r, cc]["beta"][h], t[r, cc]["eg"][h]
                vh = v_ref[r, rows_of[cc], hsl[h]].astype(F32)
                rhs.append(jnp.concatenate([vh * beta, kh[(r, cc), h].astype(F32) * (beta * eg)], axis=1))
            rhs = jnp.concatenate(rhs, axis=0).astype(BF16)
            sols[st] = _dot(block_diag(ps[st].astype(BF16)), rhs)

    def sol_rows(it, h):
        return sols[it, h // gh][(h % gh) * c:(h % gh + 1) * c]

    ws = {}

    def read_state(cc):
        rows = rows_of[cc]
        for r in range(nr):
            for a in range(n_heads // 2):
                pair = (2 * a, 2 * a + 1)
                w_row = jnp.concatenate([sol_rows((r, cc), h)[:, HEAD:] for h in pair], axis=1)
                qd_row = jnp.concatenate(
                    [q_ref[r, rows, hsl[h]].astype(F32) * (t[r, cc]["eg"][h] * scale) for h in pair],
                    axis=1)
                lhs = jnp.concatenate([w_row, qd_row], axis=0).astype(BF16)
                s0, s1 = (states[r, h].astype(BF16) for h in pair)
                rhs = jnp.concatenate([jnp.concatenate([s0, zero_state], axis=1),
                                       jnp.concatenate([zero_state, s1], axis=1)], axis=0)
                ws[cc, r, a] = _dot(lhs, rhs)

    def update_state(cc):
        rows = rows_of[cc]

        def ws_of(r, h):
            return ws[cc, r, h // 2][:, (h % 2) * HEAD:(h % 2 + 1) * HEAD]

        for r in range(nr):
            for g in groups:
                hs = range(g * gh, (g + 1) * gh)
                vn = [(sol_rows((r, cc), h)[:, :HEAD] - ws_of(r, h)[:c]).astype(BF16) for h in hs]
                rhs = jnp.concatenate([lane_blocks(vn[i], i, zero_blk) for i in range(gh)], axis=0)
                kd = [kh[(r, cc), h].astype(F32) * t[r, cc]["ekd"][h] for h in hs]
                kdt = jnp.concatenate([jnp.concatenate([kd[2 * i], kd[2 * i + 1]], axis=0).T
                                       for i in range(gh // 2)], axis=1)
                lhs = jnp.concatenate([qkm[(r, cc), g], kdt.astype(BF16)], axis=0)
                out = _dot(lhs, rhs)
                for i, h in enumerate(hs):
                    osl = slice(i * HEAD, (i + 1) * HEAD)
                    o = ws_of(r, h)[c:] + out[:c, osl]
                    gl = t[r, cc]["egl"][:, n_heads + h:n_heads + h + 1]
                    states[r, h] = states[r, h] * gl + out[c:, osl]
                    on = o * lax.rsqrt(jnp.mean(o * o, axis=-1, keepdims=True) + RMS_EPS) * onw_ref[...]
                    store_out(r, rows, hsl[h], (on * gs_ref[r, rows, hsl[h]].astype(F32)).astype(BF16))

    def save_state():
        for r in range(nr):
            for h in heads:
                s_scr[r, h] = states[r, h]

    quarter = len(streams) // 4
    parts = [streams[i * quarter:(i + 1) * quarter] for i in range(4)]
    tasks = [prepare] + [functools.partial(scores, p) for p in parts] + [square]
    sq = c // 2
    while sq > 2:
        tasks.append(square_and_extend)
        sq //= 2
    tasks += [extend] + [functools.partial(solve, p) for p in parts]
    for cc in range(nc):
        tasks += [functools.partial(read_state, cc), functools.partial(update_state, cc)]
    return tasks + [save_state]


def _mix_out_tasks(h_ref, n_ref, og_ref, wb_ref, lnw_ref, lnb_ref, ws_ref, bs_ref, wo_ref, o_ref,
                   gu_scr, vln_scr, sg_scr, mg_scr):
    nr, ts, d = h_ref.shape
    n_groups = d // HEAD
    gc = GM_CHUNK
    blk = 2 * LANE

    def n_all():
        return jnp.concatenate([n_ref[r] for r in range(nr)], axis=0)

    def gmlp_u(j):
        cols = slice(j * blk, (j + 1) * blk)
        gu_scr[:, cols] = _gelu(_dot(n_all(), wb_ref[:, cols])).astype(BF16)

    def gmlp_v():
        gv = _gelu(_dot(n_all(), wb_ref[:, d:2 * d]))
        mu = jnp.mean(gv, axis=-1, keepdims=True)
        xc = gv - mu
        ln = xc * lax.rsqrt(jnp.mean(xc * xc, axis=-1, keepdims=True) + LN_EPS)
        vln_scr[...] = (ln * lnw_ref[...] + lnb_ref[...]).astype(BF16)

    def gate_unit(h):
        pos_r = lax.broadcasted_iota(jnp.int32, (gc, gc), 0) // CHUNK
        pos_c = lax.broadcasted_iota(jnp.int32, (gc, gc), 1) // CHUNK
        hs = slice(h * HEAD, (h + 1) * HEAD)
        w_m = jnp.where(pos_c <= pos_r, ws_ref[h], 0.0).astype(BF16)
        bias = bs_ref[:, h:h + 1]
        for g in range(nr * ts // gc):
            rows = slice(g * gc, (g + 1) * gc)
            mixed = _dot(w_m, vln_scr[rows, hs]) + bias
            sg_scr[rows, hs] = (gu_scr[rows, hs].astype(F32) * mixed).astype(BF16)

    def merge(j):
        cols = slice(j * blk, (j + 1) * blk)
        n = n_all()
        gate = lambda base: jax.nn.sigmoid(_dot(n, wb_ref[:, base + j * blk:base + (j + 1) * blk]))
        y_a = _dot(og_ref[...], wo_ref[:, j * blk:(j + 1) * blk])
        y_b = _dot(sg_scr[...], wo_ref[:, d + j * blk:d + (j + 1) * blk])
        mg_scr[:, cols] = (gate(2 * d) * y_a + gate(3 * d) * y_b).astype(BF16)

    def project_out(j):
        cols = slice(j * blk, (j + 1) * blk)
        y = _dot(mg_scr[...], wo_ref[:, 2 * d + j * blk:2 * d + (j + 1) * blk])
        for r in range(nr):
            o_ref[r, :, cols] = h_ref[r, :, cols] + y[r * ts:(r + 1) * ts]

    return ([gmlp_v] + [functools.partial(gmlp_u, j) for j in range(d // blk)]
            + [functools.partial(gate_unit, h) for h in range(n_groups)]
            + [functools.partial(merge, j) for j in range(d // blk)]
            + [functools.partial(project_out, j) for j in range(d // blk)])


def _delta_mix_kernel(q_ref, k_ref, v_ref, ba_ref, gs_ref, h_ref, n_ref, hp_ref, onw_ref, wb_ref,
                      lnw_ref, lnb_ref, ws_ref, bs_ref, wo_ref, o_ref,
                      s_scr, og_scr, gu_scr, vln_scr, sg_scr, mg_scr, *, tiles_per_seq):
    i = pl.program_id(0)
    nr, ts, d = q_ref.shape

    @pl.when(i == 0)
    def _():
        og_scr[...] = jnp.zeros(og_scr.shape, BF16)

    @pl.when(lax.rem(i, tiles_per_seq) == 0)
    def _():
        s_scr[...] = jnp.zeros(s_scr.shape, F32)

    slot = lax.rem(i, 2)

    def store_out(r, rows, lanes, value):
        og_scr[slot, r * ts + rows.start:r * ts + rows.stop, lanes] = value

    _interleave(_delta_tasks(q_ref, k_ref, v_ref, ba_ref, gs_ref, hp_ref, onw_ref, s_scr, store_out),
                _mix_out_tasks(h_ref, n_ref, og_scr.at[1 - slot], wb_ref, lnw_ref, lnb_ref, ws_ref,
                               bs_ref, wo_ref, o_ref, gu_scr, vln_scr, sg_scr, mg_scr))


def _delta_mix(q, k, v, ba, gs, h, n, hp, onw, wb, ln_w, ln_b, w_s, b_s_t, wo):
    b, s, d = q.shape
    assert HEAD == LANE and 2 * LANE % CHUNK == 0 and (d // HEAD) % (2 * LANE // CHUNK) == 0
    ts, nr = DELTA_TOKENS, DELTA_ROWS
    assert b % nr == 0 and s % ts == 0 and ts % CHUNK == 0 and ts % GM_CHUNK == 0
    tps = s // ts
    nt = (b // nr) * tps

    def at(step):
        return (step // tps, step % tps, 0)

    cur = lambda w: pl.BlockSpec((nr, ts, w), lambda i: at(jnp.minimum(i, nt - 1)))
    lag = lambda w: pl.BlockSpec((nr, ts, w), lambda i: at(jnp.maximum(i - 1, 0)))
    weights = (hp, onw, wb, ln_w, ln_b, w_s, b_s_t, wo)
    rows = nr * ts
    return pl.pallas_call(
        functools.partial(_delta_mix_kernel, tiles_per_seq=tps),
        grid=(nt + 1,),
        in_specs=[cur(d), cur(d), cur(d), cur(LANE), cur(d), lag(d), lag(d)]
                 + [_resident(w.shape) for w in weights],
        out_specs=lag(d),
        out_shape=jax.ShapeDtypeStruct((b, s, d), F32),
        scratch_shapes=[pltpu.VMEM((nr, d // HEAD, HEAD, HEAD), F32), pltpu.VMEM((2, rows, d), BF16)]
                       + [pltpu.VMEM((rows, d), BF16)] * 4,
        compiler_params=pltpu.CompilerParams(dimension_semantics=("arbitrary",),
                                             vmem_limit_bytes=VMEM_LIMIT),
        name="delta_mix_out",
    )(q, k, v, ba, gs, h, n, *weights)


def _layer(h, p):
    b, s, d = h.shape
    n_heads = d // HEAD
    row = lambda a: a.reshape(1, -1).astype(F32)
    bf = lambda a: a.astype(BF16)

    w_in = p["w_in"]
    split = 4 * d + 2 * n_heads
    w_a = _pad_cols(bf(w_in[:, :split]), 4 * d + LANE)
    w_b = _pad_cols(bf(w_in[:, split:]), 4 * d)
    outs = _ffn_mix(h.reshape(b * s, d), s, row(p["ffn1_norm_w"]), bf(p["ffn1_w_gate"]),
                    bf(p["ffn1_w_up"]), bf(p["ffn1_w_down"]), row(p["mix_norm_w"]), w_a,
                    p["dn_conv_w"].astype(F32))
    h1, n, q, k, v, gs, ba = (o.reshape(b, s, -1) for o in outs)

    hp = jnp.zeros((2, LANE), F32)
    hp = hp.at[0, n_heads:2 * n_heads].set(p["dn_a_log"].astype(F32))
    hp = hp.at[1, n_heads:2 * n_heads].set(p["dn_dt_bias"].astype(F32))
    w_o = _pad_cols(jnp.concatenate([bf(p["dn_w_o"]), bf(p["gmlp_w_o"]), bf(p["w_out"])], axis=1), 3 * d)
    return _delta_mix(q, k, v, ba, gs, h1, n, hp, row(p["dn_out_norm_w"]), w_b,
                      row(p["sgu_norm_w"]), row(p["sgu_norm_b"]), p["sgu_w_s"].astype(F32),
                      jnp.swapaxes(p["sgu_b"], 0, 1).astype(F32), w_o)


def kernel(x, ffn1_norm_w, ffn1_w_gate, ffn1_w_up, ffn1_w_down, mix_norm_w, w_in, dn_conv_w, dn_a_log, dn_dt_bias, dn_out_norm_w, dn_w_o, sgu_norm_w, sgu_norm_b, sgu_w_s, sgu_b, gmlp_w_o, w_out, ffn2_norm_w, ffn2_w_gate, ffn2_w_up, ffn2_w_down, final_norm_w):
    assert ffn1_norm_w.shape[0] == 1, "the call pipeline is wired for the fixed depth of one layer"
    b, s, d = x.shape
    l = 0
    p = dict(ffn1_norm_w=ffn1_norm_w[l], ffn1_w_gate=ffn1_w_gate[l], ffn1_w_up=ffn1_w_up[l],
             ffn1_w_down=ffn1_w_down[l], mix_norm_w=mix_norm_w[l], w_in=w_in[l],
             dn_conv_w=dn_conv_w[l], dn_a_log=dn_a_log[l], dn_dt_bias=dn_dt_bias[l],
             dn_out_norm_w=dn_out_norm_w[l], dn_w_o=dn_w_o[l], sgu_norm_w=sgu_norm_w[l],
             sgu_norm_b=sgu_norm_b[l], sgu_w_s=sgu_w_s[l], sgu_b=sgu_b[l], gmlp_w_o=gmlp_w_o[l],
             w_out=w_out[l])
    h2 = _layer(x, p)
    bf = lambda a: a.astype(BF16)
    out = _ffn_final(h2.reshape(b * s, d), ffn2_norm_w[l].reshape(1, d), bf(ffn2_w_gate[l]),
                     bf(ffn2_w_up[l]), bf(ffn2_w_down[l]), final_norm_w.reshape(1, d))
    return out.reshape(b, s, d)
```

```python
import functools

import jax
import jax.numpy as jnp
from jax import lax
from jax.experimental import pallas as pl
from jax.experimental.pallas import tpu as pltpu

F32 = jnp.float32
BF16 = jnp.bfloat16

RMS_EPS = 1e-6
LN_EPS = 1e-5
L2_EPS = 1e-6
CHUNK = 64
GM_CHUNK = 128
HEAD = 128
CONV_K = 4
LANE = 128
CARRY = 8

FFN_TOKENS = 1024
FUSED_TOKENS = 256
FFN_COLS = 256
CONV_COLS = 256
DELTA_TOKENS = 256
DELTA_ROWS = 2
VMEM_LIMIT = 56 * 1024 * 1024


def _resident(shape):
    nd = len(shape)
    return pl.BlockSpec(shape, lambda *_: (0,) * nd, pipeline_mode=pl.Buffered(1))


def _pad_cols(w, used):
    tiles = -(-used // LANE)
    tiles += 1 - tiles % 2
    return jnp.pad(w, ((0, 0), (0, tiles * LANE - w.shape[1])))


def _dot(a, b):
    return jnp.dot(a, b, preferred_element_type=F32)


def _rms(x, w):
    return x * lax.rsqrt(jnp.mean(x * x, axis=-1, keepdims=True) + RMS_EPS) * w


def _silu(x):
    return x * jax.nn.sigmoid(x)


def _gelu(x):
    return 0.5 * x * (1.0 + lax.erf(x * 0.7071067811865476))


def _swiglu_tasks(x_ref, nw_ref, wg_ref, wu_ref, wd_ref, n_scr, a_scr, finish):
    d_ff = wg_ref.shape[1]

    def norm():
        n_scr[...] = _rms(x_ref[...], nw_ref[...]).astype(BF16)

    def chunk(j):
        sl = slice(j * FFN_COLS, (j + 1) * FFN_COLS)
        n = n_scr[...]
        g = _dot(n, wg_ref[:, sl])
        u = _dot(n, wu_ref[:, sl])
        a_scr[:, sl] = (_silu(g) * u).astype(BF16)

    def down():
        finish(x_ref[...] + 0.5 * _dot(a_scr[...], wd_ref[:, 0:x_ref.shape[1]]))

    return [norm] + [functools.partial(chunk, j) for j in range(d_ff // FFN_COLS)] + [down]


def _mix_in_tasks(n_ref, w_ref, cw_ref, carry_scr, q_ref, k_ref, v_ref, gs_ref, ba_ref):
    ts, d = n_ref.shape
    assert CONV_K == 4 and cw_ref.shape[0] == CONV_K
    n_heads = d // HEAD
    proj = {}

    def project(part):
        proj[part] = _dot(n_ref[...], w_ref[:, part * d:(part + 1) * d])

    def conv(j):
        sl = slice(j * CONV_COLS, (j + 1) * CONV_COLS)
        off = (j * CONV_COLS) % d
        x = proj[j * CONV_COLS // d][:, off:off + CONV_COLS]
        w0, w1, w2, w3 = (cw_ref[kk:kk + 1, sl] for kk in range(CONV_K))
        xe = jnp.concatenate([carry_scr[:, sl], x], axis=0)
        carry_scr[:, sl] = x[ts - CARRY:, :]
        x2 = pltpu.roll(xe, 2, axis=0)
        acc = w3 * xe + w1 * x2 + pltpu.roll(w2 * xe + w0 * x2, 1, axis=0)
        y = _silu(acc[CARRY:, :])
        for i in range(CONV_COLS // HEAD):
            head = j * (CONV_COLS // HEAD) + i
            yh = y[:, i * HEAD:(i + 1) * HEAD]
            if head < 2 * n_heads:
                yh = yh * lax.rsqrt(jnp.sum(yh * yh, axis=-1, keepdims=True) + L2_EPS)
            dst = (q_ref, k_ref, v_ref)[head // n_heads]
            hs = slice((head % n_heads) * HEAD, (head % n_heads + 1) * HEAD)
            dst[:, hs] = yh.astype(BF16)

    def gates():
        n = n_ref[...]
        gs_ref[...] = _silu(_dot(n, w_ref[:, 3 * d:4 * d])).astype(BF16)
        ba_ref[...] = _dot(n, w_ref[:, 4 * d:4 * d + LANE])

    tasks = []
    for part in range(3):
        tasks.append(functools.partial(project, part))
        tasks += [functools.partial(conv, part * (d // CONV_COLS) + j) for j in range(d // CONV_COLS)]
    return tasks + [gates]


def _interleave(a, b):
    order = sorted([((i + 0.5) / len(a), 0, i) for i in range(len(a))]
                   + [((i + 0.5) / len(b), 1, i) for i in range(len(b))])
    for _, which, i in order:
        (a, b)[which][i]()


def _ffn_mix_kernel(x_ref, nw_ref, wg_ref, wu_ref, wd_ref, ow_ref, wa_ref, cw_ref,
                    h_ref, nn_ref, q_ref, k_ref, v_ref, gs_ref, ba_ref,
                    n_scr, a_scr, prev_scr, carry_scr, *, tiles_per_seq):
    i = pl.program_id(0)

    @pl.when(i == 0)
    def _():
        prev_scr[...] = jnp.zeros(prev_scr.shape, BF16)

    @pl.when(lax.rem(i + tiles_per_seq - 1, tiles_per_seq) == 0)
    def _():
        carry_scr[...] = jnp.zeros(carry_scr.shape, F32)

    slot = lax.rem(i, 2)

    def finish(h):
        normed = _rms(h, ow_ref[...]).astype(BF16)
        h_ref[...] = h
        nn_ref[...] = normed
        prev_scr[slot] = normed

    _interleave(_swiglu_tasks(x_ref, nw_ref, wg_ref, wu_ref, wd_ref, n_scr, a_scr, finish),
                _mix_in_tasks(prev_scr.at[1 - slot], wa_ref, cw_ref, carry_scr,
                              q_ref, k_ref, v_ref, gs_ref, ba_ref))


def _ffn_mix(x2d, seq_len, norm_w, wg, wu, wd, out_norm_w, w_a, conv_w):
    t, d = x2d.shape
    tm = FUSED_TOKENS
    assert seq_len % tm == 0 and t % seq_len == 0
    nt = t // tm
    wd = _pad_cols(wd, d)
    cur = lambda w: pl.BlockSpec((tm, w), lambda i: (jnp.minimum(i, nt - 1), 0))
    lag = lambda w: pl.BlockSpec((tm, w), lambda i: (jnp.maximum(i - 1, 0), 0))
    bf = lambda w: jax.ShapeDtypeStruct((t, w), BF16)
    weights = (norm_w, wg, wu, wd, out_norm_w, w_a, conv_w)
    return pl.pallas_call(
        functools.partial(_ffn_mix_kernel, tiles_per_seq=seq_len // tm),
        grid=(nt + 1,),
        in_specs=[cur(d)] + [_resident(w.shape) for w in weights],
        out_specs=[cur(d), cur(d), lag(d), lag(d), lag(d), lag(d), lag(LANE)],
        out_shape=[jax.ShapeDtypeStruct((t, d), F32), bf(d), bf(d), bf(d), bf(d), bf(d),
                   jax.ShapeDtypeStruct((t, LANE), F32)],
        scratch_shapes=[pltpu.VMEM((tm, d), BF16), pltpu.VMEM((tm, wg.shape[1]), BF16),
                        pltpu.VMEM((2, tm, d), BF16), pltpu.VMEM((CARRY, 3 * d), F32)],
        compiler_params=pltpu.CompilerParams(dimension_semantics=("arbitrary",),
                                             vmem_limit_bytes=VMEM_LIMIT),
        name="ffn_mix_in",
    )(x2d, *weights)


def _ffn_final_kernel(x_ref, nw_ref, wg_ref, wu_ref, wd_ref, ow_ref, o_ref, n_scr, a_scr):
    def finish(h):
        o_ref[...] = _rms(h, ow_ref[...])

    for task in _swiglu_tasks(x_ref, nw_ref, wg_ref, wu_ref, wd_ref, n_scr, a_scr, finish):
        task()


def _ffn_final(x2d, norm_w, wg, wu, wd, out_norm_w):
    t, d = x2d.shape
    tm = FFN_TOKENS
    wd = _pad_cols(wd, d)
    row = pl.BlockSpec((tm, d), lambda i: (i, 0))
    weights = (norm_w, wg, wu, wd, out_norm_w)
    return pl.pallas_call(
        _ffn_final_kernel,
        grid=(t // tm,),
        in_specs=[row] + [_resident(w.shape) for w in weights],
        out_specs=row,
        out_shape=jax.ShapeDtypeStruct((t, d), F32),
        scratch_shapes=[pltpu.VMEM((tm, d), BF16), pltpu.VMEM((tm, wg.shape[1]), BF16)],
        compiler_params=pltpu.CompilerParams(dimension_semantics=("arbitrary",),
                                             vmem_limit_bytes=VMEM_LIMIT),
        name="ffn_final",
    )(x2d, *weights)


def _delta_tasks(q_ref, k_ref, v_ref, ba_ref, gs_ref, hp_ref, onw_ref, s_scr, store_out):
    nr, ts, d = q_ref.shape
    n_heads = d // HEAD
    c = CHUNK
    nc = ts // c
    gh = 2 * LANE // c
    n_groups = n_heads // gh
    heads = range(n_heads)
    groups = range(n_groups)
    hsl = [slice(h * HEAD, (h + 1) * HEAD) for h in heads]

    row = lax.broadcasted_iota(jnp.int32, (c, c), 0)
    col = lax.broadcasted_iota(jnp.int32, (c, c), 1)
    tri = (row >= col).astype(F32)
    prow = lax.broadcasted_iota(jnp.int32, (c, gh * c), 0)
    plane = lax.broadcasted_iota(jnp.int32, (c, gh * c), 1)
    pcol = plane % c
    causal = prow >= pcol
    strict = prow > pcol
    diag = prow == pcol
    eye = diag.astype(F32)
    head_of_lane = plane // c
    low_half = lax.broadcasted_iota(jnp.int32, (c, LANE), 1) < c
    scale = HEAD ** -0.5
    zero_blk = jnp.zeros((c, HEAD), BF16)
    zero_state = jnp.zeros((HEAD, HEAD), BF16)
    nt = (((1,), (1,)), ((), ()))

    def block_diag(x):
        return jnp.concatenate([jnp.where(head_of_lane == h, x, jnp.zeros_like(x)) for h in range(gh)],
                               axis=0)

    def lane_blocks(block, h, zero):
        return jnp.concatenate([block if j == h else zero for j in range(gh)], axis=1)

    def pack(cols):
        return jnp.concatenate([jnp.where(low_half, cols[2 * i], cols[2 * i + 1])
                                for i in range(gh // 2)], axis=1)

    def decay_terms(r, rows):
        ba = ba_ref[r, rows, :]
        z = ba + hp_ref[1:2, :]
        softplus = jnp.maximum(z, 0.0) + jnp.log1p(jnp.exp(-jnp.abs(z)))
        la = -jnp.exp(hp_ref[0:1, :]) * softplus
        g_all = jnp.dot(tri, la, preferred_element_type=F32, precision=lax.Precision.HIGHEST)
        g_last = g_all[c - 1:c, :]
        bc = lambda x, lane: jnp.broadcast_to(x[:, lane:lane + 1], (c, LANE))
        eg_all = jnp.exp(g_all)
        ekd_all = jnp.exp(g_last - g_all)
        return dict(beta=[bc(jax.nn.sigmoid(ba), h) for h in heads],
                    g=[bc(g_all, n_heads + h) for h in heads],
                    eg=[bc(eg_all, n_heads + h) for h in heads],
                    ekd=[bc(ekd_all, n_heads + h) for h in heads],
                    egl=jnp.exp(g_last))

    items = [(r, cc) for cc in range(nc) for r in range(nr)]
    rows_of = {cc: slice(cc * c, (cc + 1) * c) for cc in range(nc)}
    streams = [(it, g) for it in items for g in groups]
    t, kh, qkm, ms, ps, sols, states = {}, {}, {}, {}, {}, {}, {}

    def prepare():
        for it in items:
            t[it] = decay_terms(it[0], rows_of[it[1]])
            for h in heads:
                kh[it, h] = k_ref[it[0], rows_of[it[1]], hsl[h]]
        for r in range(nr):
            for h in heads:
                states[r, h] = s_scr[r, h]

    def scores(part):
        for st in part:
            (r, cc), g = st
            hs = range(g * gh, (g + 1) * gh)
            gsl = slice(g * gh * HEAD, (g + 1) * gh * HEAD)
            lhs = jnp.concatenate([q_ref[r, rows_of[cc], gsl], k_ref[r, rows_of[cc], gsl]], axis=0)
            bd_k = jnp.concatenate([lane_blocks(kh[(r, cc), h], h - g * gh, zero_blk) for h in hs], axis=0)
            qk_kk = lax.dot_general(lhs, bd_k, nt, preferred_element_type=F32)
            gcol = pack([t[r, cc]["g"][h] for h in hs])
            grow = jnp.sum(jnp.where(diag, gcol, 0.0), axis=0, keepdims=True)
            decay = jnp.where(causal, jnp.exp(jnp.where(causal, gcol - grow, 0.0)), 0.0)
            qkm[st] = (qk_kk[:c] * (decay * scale)).astype(BF16)
            lmat = jnp.where(strict, pack([t[r, cc]["beta"][h] for h in hs]) * qk_kk[c:] * decay, 0.0)
            ms[st] = (-lmat).astype(BF16)
            ps[st] = eye - lmat
    def square(part):
        for st in part:
            ms[st] = _dot(ms[st], block_diag(ms[st])).astype(BF16)

    def square_and_extend(part):
        for st in part:
            prod = _dot(jnp.concatenate([ms[st], ps[st].astype(BF16)], axis=0), block_diag(ms[st]))
            ms[st] = prod[:c].astype(BF16)
            ps[st] = ps[st] + prod[c:]

    def extend(part):
        for st in part:
            ps[st] = ps[st] + _dot(ps[st].astype(BF16), block_diag(ms[st]))

    def solve(part):
        for st in part:
            (r, cc), g = st
            rhs = []
            for h in range(g * gh, (g + 1) * gh):
                beta, eg = t[r, cc]["beta"][h], t[r, cc]["eg"][h]
                vh = v_ref[r, rows_of[cc], hsl[h]].astype(F32)
                rhs.append(jnp.concatenate([vh * beta, kh[(r, cc), h].astype(F32) * (beta * eg)], axis=1))
            rhs = jnp.concatenate(rhs, axis=0).astype(BF16)
            sols[st] = _dot(block_diag(ps[st].astype(BF16)), rhs)

    def sol_rows(it, h):
        return sols[it, h // gh][(h % gh) * c:(h % gh + 1) * c]

    ws = {}

    def read_state(cc):
        rows = rows_of[cc]
        for r in range(nr):
            for a in range(n_heads // 2):
                pair = (2 * a, 2 * a + 1)
                w_row = jnp.concatenate([sol_rows((r, cc), h)[:, HEAD:] for h in pair], axis=1)
                qd_row = jnp.concatenate(
                    [q_ref[r, rows, hsl[h]].astype(F32) * (t[r, cc]["eg"][h] * scale) for h in pair],
                    axis=1)
                lhs = jnp.concatenate([w_row, qd_row], axis=0).astype(BF16)
                s0, s1 = (states[r, h].astype(BF16) for h in pair)
                rhs = jnp.concatenate([jnp.concatenate([s0, zero_state], axis=1),
                                       jnp.concatenate([zero_state, s1], axis=1)], axis=0)
                ws[cc, r, a] = _dot(lhs, rhs)

    def update_state(cc):
        rows = rows_of[cc]

        def ws_of(r, h):
            return ws[cc, r, h // 2][:, (h % 2) * HEAD:(h % 2 + 1) * HEAD]

        for r in range(nr):
            for g in groups:
                hs = range(g * gh, (g + 1) * gh)
                vn = [(sol_rows((r, cc), h)[:, :HEAD] - ws_of(r, h)[:c]).astype(BF16) for h in hs]
                rhs = jnp.concatenate([lane_blocks(vn[i], i, zero_blk) for i in range(gh)], axis=0)
                kd = [kh[(r, cc), h].astype(F32) * t[r, cc]["ekd"][h] for h in hs]
                kdt = jnp.concatenate([jnp.concatenate([kd[2 * i], kd[2 * i + 1]], axis=0).T
                                       for i in range(gh // 2)], axis=1)
                lhs = jnp.concatenate([qkm[(r, cc), g], kdt.astype(BF16)], axis=0)
                out = _dot(lhs, rhs)
                for i, h in enumerate(hs):
                    osl = slice(i * HEAD, (i + 1) * HEAD)
                    o = ws_of(r, h)[c:] + out[:c, osl]
                    gl = t[r, cc]["egl"][:, n_heads + h:n_heads + h + 1]
                    states[r, h] = states[r, h] * gl + out[c:, osl]
                    on = o * lax.rsqrt(jnp.mean(o * o, axis=-1, keepdims=True) + RMS_EPS) * onw_ref[...]
                    store_out(r, rows, hsl[h], (on * gs_ref[r, rows, hsl[h]].astype(F32)).astype(BF16))

    def save_state():
        for r in range(nr):
            for h in heads:
                s_scr[r, h] = states[r, h]

    quarter = len(streams) // 4
    quarters = [streams[i * quarter:(i + 1) * quarter] for i in range(4)]
    tasks = [prepare] + [functools.partial(scores, p) for p in quarters]
    tasks.append(functools.partial(square, streams))
    sq = c // 2
    while sq > 2:
        tasks.append(functools.partial(square_and_extend, streams))
        sq //= 2
    tasks.append(functools.partial(extend, streams))
    tasks += [functools.partial(solve, p) for p in quarters]
    for cc in range(nc):
        tasks += [functools.partial(read_state, cc), functools.partial(update_state, cc)]
    return tasks + [save_state]


def _mix_out_tasks(h_ref, n_ref, og_ref, wb_ref, lnw_ref, lnb_ref, ws_ref, bs_ref, wo_ref, o_ref,
                   gu_scr, vln_scr, sg_scr, mg_scr):
    nr, ts, d = h_ref.shape
    n_groups = d // HEAD
    gc = GM_CHUNK
    blk = 2 * LANE

    def n_all():
        return jnp.concatenate([n_ref[r] for r in range(nr)], axis=0)

    def gmlp_u(j):
        cols = slice(j * blk, (j + 1) * blk)
        gu_scr[:, cols] = _gelu(_dot(n_all(), wb_ref[:, cols])).astype(BF16)

    def gmlp_v():
        gv = _gelu(_dot(n_all(), wb_ref[:, d:2 * d]))
        mu = jnp.mean(gv, axis=-1, keepdims=True)
        xc = gv - mu
        ln = xc * lax.rsqrt(jnp.mean(xc * xc, axis=-1, keepdims=True) + LN_EPS)
        vln_scr[...] = (ln * lnw_ref[...] + lnb_ref[...]).astype(BF16)

    def gate_unit(h):
        pos_r = lax.broadcasted_iota(jnp.int32, (gc, gc), 0) // CHUNK
        pos_c = lax.broadcasted_iota(jnp.int32, (gc, gc), 1) // CHUNK
        hs = slice(h * HEAD, (h + 1) * HEAD)
        w_m = jnp.where(pos_c <= pos_r, ws_ref[h], 0.0).astype(BF16)
        bias = bs_ref[:, h:h + 1]
        for g in range(nr * ts // gc):
            rows = slice(g * gc, (g + 1) * gc)
            mixed = _dot(w_m, vln_scr[rows, hs]) + bias
            sg_scr[rows, hs] = (gu_scr[rows, hs].astype(F32) * mixed).astype(BF16)

    def merge(j):
        cols = slice(j * blk, (j + 1) * blk)
        n = n_all()
        gate = lambda base: jax.nn.sigmoid(_dot(n, wb_ref[:, base + j * blk:base + (j + 1) * blk]))
        y_a = _dot(og_ref[...], wo_ref[:, j * blk:(j + 1) * blk])
        y_b = _dot(sg_scr[...], wo_ref[:, d + j * blk:d + (j + 1) * blk])
        mg_scr[:, cols] = (gate(2 * d) * y_a + gate(3 * d) * y_b).astype(BF16)

    def project_out(j):
        cols = slice(j * blk, (j + 1) * blk)
        y = _dot(mg_scr[...], wo_ref[:, 2 * d + j * blk:2 * d + (j + 1) * blk])
        for r in range(nr):
            o_ref[r, :, cols] = h_ref[r, :, cols] + y[r * ts:(r + 1) * ts]

    return ([gmlp_v] + [functools.partial(gmlp_u, j) for j in range(d // blk)]
            + [functools.partial(gate_unit, h) for h in range(n_groups)]
            + [functools.partial(merge, j) for j in range(d // blk)]
            + [functools.partial(project_out, j) for j in range(d // blk)])


def _delta_mix_kernel(q_ref, k_ref, v_ref, ba_ref, gs_ref, h_ref, n_ref, hp_ref, onw_ref, wb_ref,
                      lnw_ref, lnb_ref, ws_ref, bs_ref, wo_ref, o_ref,
                      s_scr, og_scr, gu_scr, vln_scr, sg_scr, mg_scr, *, tiles_per_seq):
    i = pl.program_id(0)
    nr, ts, d = q_ref.shape

    @pl.when(i == 0)
    def _():
        og_scr[...] = jnp.zeros(og_scr.shape, BF16)

    @pl.when(lax.rem(i, tiles_per_seq) == 0)
    def _():
        s_scr[...] = jnp.zeros(s_scr.shape, F32)

    slot = lax.rem(i, 2)

    def store_out(r, rows, lanes, value):
        og_scr[slot, r * ts + rows.start:r * ts + rows.stop, lanes] = value

    _interleave(_delta_tasks(q_ref, k_ref, v_ref, ba_ref, gs_ref, hp_ref, onw_ref, s_scr, store_out),
                _mix_out_tasks(h_ref, n_ref, og_scr.at[1 - slot], wb_ref, lnw_ref, lnb_ref, ws_ref,
                               bs_ref, wo_ref, o_ref, gu_scr, vln_scr, sg_scr, mg_scr))


def _delta_mix(q, k, v, ba, gs, h, n, hp, onw, wb, ln_w, ln_b, w_s, b_s_t, wo):
    b, s, d = q.shape
    assert HEAD == LANE and 2 * LANE % CHUNK == 0 and (d // HEAD) % (2 * LANE // CHUNK) == 0
    ts, nr = DELTA_TOKENS, DELTA_ROWS
    assert b % nr == 0 and s % ts == 0 and ts % CHUNK == 0 and ts % GM_CHUNK == 0
    tps = s // ts
    nt = (b // nr) * tps

    def at(step):
        return (step // tps, step % tps, 0)

    cur = lambda w: pl.BlockSpec((nr, ts, w), lambda i: at(jnp.minimum(i, nt - 1)))
    lag = lambda w: pl.BlockSpec((nr, ts, w), lambda i: at(jnp.maximum(i - 1, 0)))
    weights = (hp, onw, wb, ln_w, ln_b, w_s, b_s_t, wo)
    rows = nr * ts
    return pl.pallas_call(
        functools.partial(_delta_mix_kernel, tiles_per_seq=tps),
        grid=(nt + 1,),
        in_specs=[cur(d), cur(d), cur(d), cur(LANE), cur(d), lag(d), lag(d)]
                 + [_resident(w.shape) for w in weights],
        out_specs=lag(d),
        out_shape=jax.ShapeDtypeStruct((b, s, d), F32),
        scratch_shapes=[pltpu.VMEM((nr, d // HEAD, HEAD, HEAD), F32), pltpu.VMEM((2, rows, d), BF16)]
                       + [pltpu.VMEM((rows, d), BF16)] * 4,
        compiler_params=pltpu.CompilerParams(dimension_semantics=("arbitrary",),
                                             vmem_limit_bytes=VMEM_LIMIT),
        name="delta_mix_out",
    )(q, k, v, ba, gs, h, n, *weights)


def _layer(h, p):
    b, s, d = h.shape
    n_heads = d // HEAD
    row = lambda a: a.reshape(1, -1).astype(F32)
    bf = lambda a: a.astype(BF16)

    w_in = bf(p["w_in"])
    split = 4 * d + 2 * n_heads
    w_a = _pad_cols(w_in[:, :split], 4 * d + LANE)
    w_b = _pad_cols(w_in[:, split:], 4 * d)
    outs = _ffn_mix(h.reshape(b * s, d), s, row(p["ffn1_norm_w"]), bf(p["ffn1_w_gate"]),
                    bf(p["ffn1_w_up"]), bf(p["ffn1_w_down"]), row(p["mix_norm_w"]), w_a,
                    p["dn_conv_w"].astype(F32))
    h1, n, q, k, v, gs, ba = (o.reshape(b, s, -1) for o in outs)

    hp = jnp.zeros((2, LANE), F32)
    hp = hp.at[0, n_heads:2 * n_heads].set(p["dn_a_log"].astype(F32))
    hp = hp.at[1, n_heads:2 * n_heads].set(p["dn_dt_bias"].astype(F32))
    w_o = _pad_cols(jnp.concatenate([bf(p["dn_w_o"]), bf(p["gmlp_w_o"]), bf(p["w_out"])], axis=1), 3 * d)
    return _delta_mix(q, k, v, ba, gs, h1, n, hp, row(p["dn_out_norm_w"]), w_b,
                      row(p["sgu_norm_w"]), row(p["sgu_norm_b"]), p["sgu_w_s"].astype(F32),
                      jnp.swapaxes(p["sgu_b"], 0, 1).astype(F32), w_o)


def kernel(x, ffn1_norm_w, ffn1_w_gate, ffn1_w_up, ffn1_w_down, mix_norm_w, w_in, dn_conv_w, dn_a_log, dn_dt_bias, dn_out_norm_w, dn_w_o, sgu_norm_w, sgu_norm_b, sgu_w_s, sgu_b, gmlp_w_o, w_out, ffn2_norm_w, ffn2_w_gate, ffn2_w_up, ffn2_w_down, final_norm_w):
    assert ffn1_norm_w.shape[0] == 1, "the call pipeline is wired for the fixed depth of one layer"
    b, s, d = x.shape
    l = 0
    p = dict(ffn1_norm_w=ffn1_norm_w[l], ffn1_w_gate=ffn1_w_gate[l], ffn1_w_up=ffn1_w_up[l],
             ffn1_w_down=ffn1_w_down[l], mix_norm_w=mix_norm_w[l], w_in=w_in[l],
             dn_conv_w=dn_conv_w[l], dn_a_log=dn_a_log[l], dn_dt_bias=dn_dt_bias[l],
             dn_out_norm_w=dn_out_norm_w[l], dn_w_o=dn_w_o[l], sgu_norm_w=sgu_norm_w[l],
             sgu_norm_b=sgu_norm_b[l], sgu_w_s=sgu_w_s[l], sgu_b=sgu_b[l], gmlp_w_o=gmlp_w_o[l],
             w_out=w_out[l])
    h2 = _layer(x, p)
    bf = lambda a: a.astype(BF16)
    out = _ffn_final(h2.reshape(b * s, d), ffn2_norm_w[l].reshape(1, d), bf(ffn2_w_gate[l]),
                     bf(ffn2_w_up[l]), bf(ffn2_w_down[l]), final_norm_w.reshape(1, d))
    return out.reshape(b, s, d)
```

```python
import functools

import jax
import jax.numpy as jnp
from jax import lax
from jax.experimental import pallas as pl
from jax.experimental.pallas import tpu as pltpu

F32 = jnp.float32
BF16 = jnp.bfloat16

RMS_EPS = 1e-6
LN_EPS = 1e-5
L2_EPS = 1e-6
CHUNK = 64
GM_CHUNK = 128
HEAD = 128
CONV_K = 4
LANE = 128
CARRY = 8

FFN_TOKENS = 1024
FUSED_TOKENS = 256
FFN_COLS = 256
CONV_COLS = 256
DELTA_TOKENS = 256
DELTA_ROWS = 2
VMEM_LIMIT = 56 * 1024 * 1024


def _resident(shape):
    nd = len(shape)
    return pl.BlockSpec(shape, lambda *_: (0,) * nd, pipeline_mode=pl.Buffered(1))


def _pad_cols(w, used):
    tiles = -(-used // LANE)
    tiles += 1 - tiles % 2
    return jnp.pad(w, ((0, 0), (0, tiles * LANE - w.shape[1])))


def _dot(a, b):
    return jnp.dot(a, b, preferred_element_type=F32)


def _rms(x, w):
    return x * lax.rsqrt(jnp.mean(x * x, axis=-1, keepdims=True) + RMS_EPS) * w


def _silu(x):
    return x * jax.nn.sigmoid(x)


def _gelu(x):
    return 0.5 * x * (1.0 + lax.erf(x * 0.7071067811865476))


def _swiglu_tasks(x_ref, nw_ref, wg_ref, wu_ref, wd_ref, n_scr, a_scr, finish):
    d_ff = wg_ref.shape[1]

    def norm():
        n_scr[...] = _rms(x_ref[...], nw_ref[...]).astype(BF16)

    def chunk(j):
        sl = slice(j * FFN_COLS, (j + 1) * FFN_COLS)
        n = n_scr[...]
        g = _dot(n, wg_ref[:, sl])
        u = _dot(n, wu_ref[:, sl])
        a_scr[:, sl] = (_silu(g) * u).astype(BF16)

    def down():
        finish(x_ref[...] + 0.5 * _dot(a_scr[...], wd_ref[:, 0:x_ref.shape[1]]))

    return [norm] + [functools.partial(chunk, j) for j in range(d_ff // FFN_COLS)] + [down]


def _mix_in_tasks(n_ref, w_ref, cw_ref, carry_scr, qkvg_ref, ba_ref):
    ts, d = n_ref.shape
    assert CONV_K == 4 and cw_ref.shape[0] == CONV_K
    n_heads = d // HEAD
    proj = {}

    def project(part):
        proj[part] = _dot(n_ref[...], w_ref[:, part * d:(part + 1) * d])

    def conv(j):
        sl = slice(j * CONV_COLS, (j + 1) * CONV_COLS)
        off = (j * CONV_COLS) % d
        x = proj[j * CONV_COLS // d][:, off:off + CONV_COLS]
        w0, w1, w2, w3 = (cw_ref[kk:kk + 1, sl] for kk in range(CONV_K))
        xe = jnp.concatenate([carry_scr[:, sl], x], axis=0)
        carry_scr[:, sl] = x[ts - CARRY:, :]
        x2 = pltpu.roll(xe, 2, axis=0)
        acc = w3 * xe + w1 * x2 + pltpu.roll(w2 * xe + w0 * x2, 1, axis=0)
        y = _silu(acc[CARRY:, :])
        for i in range(CONV_COLS // HEAD):
            head = j * (CONV_COLS // HEAD) + i
            yh = y[:, i * HEAD:(i + 1) * HEAD]
            if head < 2 * n_heads:
                yh = yh * lax.rsqrt(jnp.sum(yh * yh, axis=-1, keepdims=True) + L2_EPS)
            qkvg_ref[:, head * HEAD:(head + 1) * HEAD] = yh.astype(BF16)

    def gates():
        n = n_ref[...]
        qkvg_ref[:, 3 * d:4 * d] = _silu(_dot(n, w_ref[:, 3 * d:4 * d])).astype(BF16)
        ba_ref[...] = _dot(n, w_ref[:, 4 * d:4 * d + LANE])

    tasks = []
    for part in range(3):
        tasks.append(functools.partial(project, part))
        tasks += [functools.partial(conv, part * (d // CONV_COLS) + j) for j in range(d // CONV_COLS)]
    return tasks + [gates]


def _interleave(a, b):
    order = sorted([((i + 0.5) / len(a), 0, i) for i in range(len(a))]
                   + [((i + 0.5) / len(b), 1, i) for i in range(len(b))])
    for _, which, i in order:
        (a, b)[which][i]()


def _ffn_mix_kernel(x_ref, nw_ref, wg_ref, wu_ref, wd_ref, ow_ref, wa_ref, cw_ref,
                    h_ref, nn_ref, qkvg_ref, ba_ref,
                    n_scr, a_scr, prev_scr, carry_scr, *, tiles_per_seq):
    i = pl.program_id(0)

    @pl.when(i == 0)
    def _():
        prev_scr[...] = jnp.zeros(prev_scr.shape, BF16)

    @pl.when(lax.rem(i + tiles_per_seq - 1, tiles_per_seq) == 0)
    def _():
        carry_scr[...] = jnp.zeros(carry_scr.shape, F32)

    slot = lax.rem(i, 2)

    def finish(h):
        normed = _rms(h, ow_ref[...]).astype(BF16)
        h_ref[...] = h
        nn_ref[...] = normed
        prev_scr[slot] = normed

    _interleave(_swiglu_tasks(x_ref, nw_ref, wg_ref, wu_ref, wd_ref, n_scr, a_scr, finish),
                _mix_in_tasks(prev_scr.at[1 - slot], wa_ref, cw_ref, carry_scr, qkvg_ref, ba_ref))


def _ffn_mix(x2d, seq_len, norm_w, wg, wu, wd, out_norm_w, w_a, conv_w):
    t, d = x2d.shape
    tm = FUSED_TOKENS
    assert seq_len % tm == 0 and t % seq_len == 0
    nt = t // tm
    wd = _pad_cols(wd, d)
    cur = lambda w: pl.BlockSpec((tm, w), lambda i: (jnp.minimum(i, nt - 1), 0))
    lag = lambda w: pl.BlockSpec((tm, w), lambda i: (jnp.maximum(i - 1, 0), 0))
    bf = lambda w: jax.ShapeDtypeStruct((t, w), BF16)
    weights = (norm_w, wg, wu, wd, out_norm_w, w_a, conv_w)
    return pl.pallas_call(
        functools.partial(_ffn_mix_kernel, tiles_per_seq=seq_len // tm),
        grid=(nt + 1,),
        in_specs=[cur(d)] + [_resident(w.shape) for w in weights],
        out_specs=[cur(d), cur(d), lag(4 * d), lag(LANE)],
        out_shape=[jax.ShapeDtypeStruct((t, d), F32), bf(d), bf(4 * d),
                   jax.ShapeDtypeStruct((t, LANE), F32)],
        scratch_shapes=[pltpu.VMEM((tm, d), BF16), pltpu.VMEM((tm, wg.shape[1]), BF16),
                        pltpu.VMEM((2, tm, d), BF16), pltpu.VMEM((CARRY, 3 * d), F32)],
        compiler_params=pltpu.CompilerParams(dimension_semantics=("arbitrary",),
                                             vmem_limit_bytes=VMEM_LIMIT),
        name="ffn_mix_in",
    )(x2d, *weights)


def _ffn_final_kernel(x_ref, nw_ref, wg_ref, wu_ref, wd_ref, ow_ref, o_ref, n_scr, a_scr):
    def finish(h):
        o_ref[...] = _rms(h, ow_ref[...])

    for task in _swiglu_tasks(x_ref, nw_ref, wg_ref, wu_ref, wd_ref, n_scr, a_scr, finish):
        task()


def _ffn_final(x2d, norm_w, wg, wu, wd, out_norm_w):
    t, d = x2d.shape
    tm = FFN_TOKENS
    wd = _pad_cols(wd, d)
    row = pl.BlockSpec((tm, d), lambda i: (i, 0))
    weights = (norm_w, wg, wu, wd, out_norm_w)
    return pl.pallas_call(
        _ffn_final_kernel,
        grid=(t // tm,),
        in_specs=[row] + [_resident(w.shape) for w in weights],
        out_specs=row,
        out_shape=jax.ShapeDtypeStruct((t, d), F32),
        scratch_shapes=[pltpu.VMEM((tm, d), BF16), pltpu.VMEM((tm, wg.shape[1]), BF16)],
        compiler_params=pltpu.CompilerParams(dimension_semantics=("arbitrary",),
                                             vmem_limit_bytes=VMEM_LIMIT),
        name="ffn_final",
    )(x2d, *weights)


def _delta_tasks(q_ref, k_ref, v_ref, ba_ref, gs_ref, hp_ref, onw_ref, s_scr, store_out):
    nr, ts, d = q_ref.shape
    n_heads = d // HEAD
    c = CHUNK
    nc = ts // c
    gh = 2 * LANE // c
    n_groups = n_heads // gh
    heads = range(n_heads)
    groups = range(n_groups)
    hsl = [slice(h * HEAD, (h + 1) * HEAD) for h in heads]

    row = lax.broadcasted_iota(jnp.int32, (c, c), 0)
    col = lax.broadcasted_iota(jnp.int32, (c, c), 1)
    tri = (row >= col).astype(F32)
    prow = lax.broadcasted_iota(jnp.int32, (c, gh * c), 0)
    plane = lax.broadcasted_iota(jnp.int32, (c, gh * c), 1)
    pcol = plane % c
    causal = prow >= pcol
    strict = prow > pcol
    diag = prow == pcol
    eye = diag.astype(F32)
    head_of_lane = plane // c
    low_half = lax.broadcasted_iota(jnp.int32, (c, LANE), 1) < c
    scale = HEAD ** -0.5
    zero_blk = jnp.zeros((c, HEAD), BF16)
    zero_state = jnp.zeros((HEAD, HEAD), BF16)
    nt = (((1,), (1,)), ((), ()))

    def block_diag(x):
        return jnp.concatenate([jnp.where(head_of_lane == h, x, jnp.zeros_like(x)) for h in range(gh)],
                               axis=0)

    def lane_blocks(block, h, zero):
        return jnp.concatenate([block if j == h else zero for j in range(gh)], axis=1)

    def pack(cols):
        return jnp.concatenate([jnp.where(low_half, cols[2 * i], cols[2 * i + 1])
                                for i in range(gh // 2)], axis=1)

    class HeadColumns:
        def __init__(self, x, first_lane):
            self.x, self.first_lane = x, first_lane

        def __getitem__(self, h):
            lane = self.first_lane + h
            return jnp.broadcast_to(self.x[:, lane:lane + 1], (c, LANE))

    def decay_terms(r, rows):
        ba = ba_ref[r, rows, :]
        z = ba + hp_ref[1:2, :]
        softplus = jnp.maximum(z, 0.0) + jnp.log1p(jnp.exp(-jnp.abs(z)))
        la = -jnp.exp(hp_ref[0:1, :]) * softplus
        g_all = jnp.dot(tri, la, preferred_element_type=F32, precision=lax.Precision.HIGHEST)
        g_last = g_all[c - 1:c, :]
        return dict(beta=HeadColumns(jax.nn.sigmoid(ba), 0), g=HeadColumns(g_all, n_heads),
                    eg=HeadColumns(jnp.exp(g_all), n_heads),
                    ekd=HeadColumns(jnp.exp(g_last - g_all), n_heads), egl=jnp.exp(g_last))

    items = [(r, cc) for cc in range(nc) for r in range(nr)]
    rows_of = {cc: slice(cc * c, (cc + 1) * c) for cc in range(nc)}
    streams = [(it, g) for it in items for g in groups]
    t, kh, qkm, ms, ps, sols, states = {}, {}, {}, {}, {}, {}, {}

    def prepare():
        for it in items:
            t[it] = decay_terms(it[0], rows_of[it[1]])
            for h in heads:
                kh[it, h] = k_ref[it[0], rows_of[it[1]], hsl[h]]
        for r in range(nr):
            for h in heads:
                states[r, h] = s_scr[r, h]

    def scores(part):
        for st in part:
            (r, cc), g = st
            hs = range(g * gh, (g + 1) * gh)
            gsl = slice(g * gh * HEAD, (g + 1) * gh * HEAD)
            lhs = jnp.concatenate([q_ref[r, rows_of[cc], gsl], k_ref[r, rows_of[cc], gsl]], axis=0)
            bd_k = jnp.concatenate([lane_blocks(kh[(r, cc), h], h - g * gh, zero_blk) for h in hs], axis=0)
            qk_kk = lax.dot_general(lhs, bd_k, nt, preferred_element_type=F32)
            gcol = pack([t[r, cc]["g"][h] for h in hs])
            grow = jnp.sum(jnp.where(diag, gcol, 0.0), axis=0, keepdims=True)
            decay = jnp.where(causal, jnp.exp(jnp.where(causal, gcol - grow, 0.0)), 0.0)
            qkm[st] = (qk_kk[:c] * (decay * scale)).astype(BF16)
            lmat = jnp.where(strict, pack([t[r, cc]["beta"][h] for h in hs]) * qk_kk[c:] * decay, 0.0)
            ms[st] = (-lmat).astype(BF16)
            ps[st] = eye - lmat
    def square(part):
        for st in part:
            ms[st] = _dot(ms[st], block_diag(ms[st])).astype(BF16)

    def square_and_extend(part):
        for st in part:
            prod = _dot(jnp.concatenate([ms[st], ps[st].astype(BF16)], axis=0), block_diag(ms[st]))
            ms[st] = prod[:c].astype(BF16)
            ps[st] = ps[st] + prod[c:]

    def extend(part):
        for st in part:
            ps[st] = ps[st] + _dot(ps[st].astype(BF16), block_diag(ms[st]))

    def solve(part):
        for st in part:
            (r, cc), g = st
            rhs = []
            for h in range(g * gh, (g + 1) * gh):
                beta, eg = t[r, cc]["beta"][h], t[r, cc]["eg"][h]
                vh = v_ref[r, rows_of[cc], hsl[h]].astype(F32)
                rhs.append(jnp.concatenate([vh * beta, kh[(r, cc), h].astype(F32) * (beta * eg)], axis=1))
            rhs = jnp.concatenate(rhs, axis=0).astype(BF16)
            sols[st] = _dot(block_diag(ps[st].astype(BF16)), rhs)

    def sol_rows(it, h):
        return sols[it, h // gh][(h % gh) * c:(h % gh + 1) * c]

    ws = {}

    def read_state(cc):
        rows = rows_of[cc]
        for r in range(nr):
            for a in range(n_heads // 2):
                pair = (2 * a, 2 * a + 1)
                w_row = jnp.concatenate([sol_rows((r, cc), h)[:, HEAD:] for h in pair], axis=1)
                qd_row = jnp.concatenate(
                    [q_ref[r, rows, hsl[h]].astype(F32) * (t[r, cc]["eg"][h] * scale) for h in pair],
                    axis=1)
                lhs = jnp.concatenate([w_row, qd_row], axis=0).astype(BF16)
                s0, s1 = (states[r, h].astype(BF16) for h in pair)
                rhs = jnp.concatenate([jnp.concatenate([s0, zero_state], axis=1),
                                       jnp.concatenate([zero_state, s1], axis=1)], axis=0)
                ws[cc, r, a] = _dot(lhs, rhs)

    def update_state(cc):
        rows = rows_of[cc]

        def ws_of(r, h):
            return ws[cc, r, h // 2][:, (h % 2) * HEAD:(h % 2 + 1) * HEAD]

        for r in range(nr):
            for g in groups:
                hs = range(g * gh, (g + 1) * gh)
                vn = [(sol_rows((r, cc), h)[:, :HEAD] - ws_of(r, h)[:c]).astype(BF16) for h in hs]
                rhs = jnp.concatenate([lane_blocks(vn[i], i, zero_blk) for i in range(gh)], axis=0)
                kd = [kh[(r, cc), h].astype(F32) * t[r, cc]["ekd"][h] for h in hs]
                kdt = jnp.concatenate([jnp.concatenate([kd[2 * i], kd[2 * i + 1]], axis=0).T
                                       for i in range(gh // 2)], axis=1)
                lhs = jnp.concatenate([qkm[(r, cc), g], kdt.astype(BF16)], axis=0)
                out = _dot(lhs, rhs)
                for i, h in enumerate(hs):
                    osl = slice(i * HEAD, (i + 1) * HEAD)
                    o = ws_of(r, h)[c:] + out[:c, osl]
                    gl = t[r, cc]["egl"][:, n_heads + h:n_heads + h + 1]
                    states[r, h] = states[r, h] * gl + out[c:, osl]
                    on = o * lax.rsqrt(jnp.mean(o * o, axis=-1, keepdims=True) + RMS_EPS) * onw_ref[...]
                    store_out(r, rows, hsl[h], (on * gs_ref[r, rows, hsl[h]].astype(F32)).astype(BF16))

    def save_state():
        for r in range(nr):
            for h in heads:
                s_scr[r, h] = states[r, h]

    quarter = len(streams) // 4
    quarters = [streams[i * quarter:(i + 1) * quarter] for i in range(4)]
    tasks = [prepare] + [functools.partial(scores, p) for p in quarters]
    tasks.append(functools.partial(square, streams))
    sq = c // 2
    while sq > 2:
        tasks.append(functools.partial(square_and_extend, streams))
        sq //= 2
    tasks.append(functools.partial(extend, streams))
    tasks += [functools.partial(solve, p) for p in quarters]
    for cc in range(nc):
        tasks += [functools.partial(read_state, cc), functools.partial(update_state, cc)]
    return tasks + [save_state]


def _mix_out_tasks(h_ref, n_ref, og_ref, wb_ref, lnw_ref, lnb_ref, ws_ref, bs_ref, wo_ref, o_ref,
                   gu_scr, vln_scr, sg_scr, mg_scr):
    nr, ts, d = h_ref.shape
    n_groups = d // HEAD
    gc = GM_CHUNK
    blk = 2 * LANE

    def n_all():
        return jnp.concatenate([n_ref[r] for r in range(nr)], axis=0)

    def gmlp_u(j):
        cols = slice(j * blk, (j + 1) * blk)
        gu_scr[:, cols] = _gelu(_dot(n_all(), wb_ref[:, cols])).astype(BF16)

    def gmlp_v():
        gv = _gelu(_dot(n_all(), wb_ref[:, d:2 * d]))
        mu = jnp.mean(gv, axis=-1, keepdims=True)
        xc = gv - mu
        ln = xc * lax.rsqrt(jnp.mean(xc * xc, axis=-1, keepdims=True) + LN_EPS)
        vln_scr[...] = (ln * lnw_ref[...] + lnb_ref[...]).astype(BF16)

    def gate_unit(h):
        pos_r = lax.broadcasted_iota(jnp.int32, (gc, gc), 0) // CHUNK
        pos_c = lax.broadcasted_iota(jnp.int32, (gc, gc), 1) // CHUNK
        hs = slice(h * HEAD, (h + 1) * HEAD)
        w_m = jnp.where(pos_c <= pos_r, ws_ref[h], 0.0).astype(BF16)
        bias = bs_ref[:, h:h + 1]
        for g in range(nr * ts // gc):
            rows = slice(g * gc, (g + 1) * gc)
            mixed = _dot(w_m, vln_scr[rows, hs]) + bias
            sg_scr[rows, hs] = (gu_scr[rows, hs].astype(F32) * mixed).astype(BF16)

    def merge(j):
        cols = slice(j * blk, (j + 1) * blk)
        n = n_all()
        gate = lambda base: jax.nn.sigmoid(_dot(n, wb_ref[:, base + j * blk:base + (j + 1) * blk]))
        y_a = _dot(og_ref[...], wo_ref[:, j * blk:(j + 1) * blk])
        y_b = _dot(sg_scr[...], wo_ref[:, d + j * blk:d + (j + 1) * blk])
        mg_scr[:, cols] = (gate(2 * d) * y_a + gate(3 * d) * y_b).astype(BF16)

    def project_out(j):
        cols = slice(j * blk, (j + 1) * blk)
        y = _dot(mg_scr[...], wo_ref[:, 2 * d + j * blk:2 * d + (j + 1) * blk])
        for r in range(nr):
            o_ref[r, :, cols] = h_ref[r, :, cols] + y[r * ts:(r + 1) * ts]

    return ([gmlp_v] + [functools.partial(gmlp_u, j) for j in range(d // blk)]
            + [functools.partial(gate_unit, h) for h in range(n_groups)]
            + [functools.partial(merge, j) for j in range(d // blk)]
            + [functools.partial(project_out, j) for j in range(d // blk)])


def _delta_mix_kernel(qkvg_ref, ba_ref, h_ref, n_ref, hp_ref, onw_ref, wb_ref,
                      lnw_ref, lnb_ref, ws_ref, bs_ref, wo_ref, o_ref,
                      s_scr, og_scr, gu_scr, vln_scr, sg_scr, mg_scr, *, tiles_per_seq):
    i = pl.program_id(0)
    nr, ts, d = h_ref.shape
    q_ref, k_ref, v_ref, gs_ref = (qkvg_ref.at[:, :, j * d:(j + 1) * d] for j in range(4))

    @pl.when(i == 0)
    def _():
        og_scr[...] = jnp.zeros(og_scr.shape, BF16)

    @pl.when(lax.rem(i, tiles_per_seq) == 0)
    def _():
        s_scr[...] = jnp.zeros(s_scr.shape, F32)

    slot = lax.rem(i, 2)

    def store_out(r, rows, lanes, value):
        og_scr[slot, r * ts + rows.start:r * ts + rows.stop, lanes] = value

    _interleave(_delta_tasks(q_ref, k_ref, v_ref, ba_ref, gs_ref, hp_ref, onw_ref, s_scr, store_out),
                _mix_out_tasks(h_ref, n_ref, og_scr.at[1 - slot], wb_ref, lnw_ref, lnb_ref, ws_ref,
                               bs_ref, wo_ref, o_ref, gu_scr, vln_scr, sg_scr, mg_scr))


def _delta_mix(qkvg, ba, h, n, hp, onw, wb, ln_w, ln_b, w_s, b_s_t, wo):
    b, s, d = h.shape
    assert HEAD == LANE and 2 * LANE % CHUNK == 0 and (d // HEAD) % (2 * LANE // CHUNK) == 0
    ts, nr = DELTA_TOKENS, DELTA_ROWS
    assert b % nr == 0 and s % ts == 0 and ts % CHUNK == 0 and ts % GM_CHUNK == 0
    tps = s // ts
    nt = (b // nr) * tps

    def at(step):
        return (step // tps, step % tps, 0)

    cur = lambda w: pl.BlockSpec((nr, ts, w), lambda i: at(jnp.minimum(i, nt - 1)))
    lag = lambda w: pl.BlockSpec((nr, ts, w), lambda i: at(jnp.maximum(i - 1, 0)))
    weights = (hp, onw, wb, ln_w, ln_b, w_s, b_s_t, wo)
    rows = nr * ts
    return pl.pallas_call(
        functools.partial(_delta_mix_kernel, tiles_per_seq=tps),
        grid=(nt + 1,),
        in_specs=[cur(4 * d), cur(LANE), lag(d), lag(d)] + [_resident(w.shape) for w in weights],
        out_specs=lag(d),
        out_shape=jax.ShapeDtypeStruct((b, s, d), F32),
        scratch_shapes=[pltpu.VMEM((nr, d // HEAD, HEAD, HEAD), F32), pltpu.VMEM((2, rows, d), BF16)]
                       + [pltpu.VMEM((rows, d), BF16)] * 4,
        compiler_params=pltpu.CompilerParams(dimension_semantics=("arbitrary",),
                                             vmem_limit_bytes=VMEM_LIMIT),
        name="delta_mix_out",
    )(qkvg, ba, h, n, *weights)


def _layer(h, p):
    b, s, d = h.shape
    n_heads = d // HEAD
    row = lambda a: a.reshape(1, -1).astype(F32)
    bf = lambda a: a.astype(BF16)

    w_in = bf(p["w_in"])
    split = 4 * d + 2 * n_heads
    w_a = _pad_cols(w_in[:, :split], 4 * d + LANE)
    w_b = _pad_cols(w_in[:, split:], 4 * d)
    outs = _ffn_mix(h.reshape(b * s, d), s, row(p["ffn1_norm_w"]), bf(p["ffn1_w_gate"]),
                    bf(p["ffn1_w_up"]), bf(p["ffn1_w_down"]), row(p["mix_norm_w"]), w_a,
                    p["dn_conv_w"].astype(F32))
    h1, n, qkvg, ba = (o.reshape(b, s, -1) for o in outs)

    hp = jnp.pad(jnp.stack([p["dn_a_log"], p["dn_dt_bias"]]).astype(F32),
                 ((0, 0), (n_heads, LANE - 2 * n_heads)))
    w_o = _pad_cols(jnp.concatenate([bf(p["dn_w_o"]), bf(p["gmlp_w_o"]), bf(p["w_out"])], axis=1), 3 * d)
    return _delta_mix(qkvg, ba, h1, n, hp, row(p["dn_out_norm_w"]), w_b,
                      row(p["sgu_norm_w"]), row(p["sgu_norm_b"]), p["sgu_w_s"].astype(F32),
                      jnp.swapaxes(p["sgu_b"], 0, 1).astype(F32), w_o)


def kernel(x, ffn1_norm_w, ffn1_w_gate, ffn1_w_up, ffn1_w_down, mix_norm_w, w_in, dn_conv_w, dn_a_log, dn_dt_bias, dn_out_norm_w, dn_w_o, sgu_norm_w, sgu_norm_b, sgu_w_s, sgu_b, gmlp_w_o, w_out, ffn2_norm_w, ffn2_w_gate, ffn2_w_up, ffn2_w_down, final_norm_w):
    assert ffn1_norm_w.shape[0] == 1, "the call pipeline is wired for the fixed depth of one layer"
    b, s, d = x.shape
    l = 0
    p = dict(ffn1_norm_w=ffn1_norm_w[l], ffn1_w_gate=ffn1_w_gate[l], ffn1_w_up=ffn1_w_up[l],
             ffn1_w_down=ffn1_w_down[l], mix_norm_w=mix_norm_w[l], w_in=w_in[l],
             dn_conv_w=dn_conv_w[l], dn_a_log=dn_a_log[l], dn_dt_bias=dn_dt_bias[l],
             dn_out_norm_w=dn_out_norm_w[l], dn_w_o=dn_w_o[l], sgu_norm_w=sgu_norm_w[l],
             sgu_norm_b=sgu_norm_b[l], sgu_w_s=sgu_w_s[l], sgu_b=sgu_b[l], gmlp_w_o=gmlp_w_o[l],
             w_out=w_out[l])
    h2 = _layer(x, p)
    bf = lambda a: a.astype(BF16)
    out = _ffn_final(h2.reshape(b * s, d), ffn2_norm_w[l].reshape(1, d), bf(ffn2_w_gate[l]),
                     bf(ffn2_w_up[l]), bf(ffn2_w_down[l]), final_norm_w.reshape(1, d))
    return out.reshape(b, s, d)
```

```python
import functools

import jax
import jax.numpy as jnp
from jax import lax
from jax.experimental import pallas as pl
from jax.experimental.pallas import tpu as pltpu

F32 = jnp.float32
BF16 = jnp.bfloat16

RMS_EPS = 1e-6
LN_EPS = 1e-5
L2_EPS = 1e-6
CHUNK = 64
GM_CHUNK = 128
HEAD = 128
CONV_K = 4
LANE = 128
CARRY = 8

FFN_TOKENS = 1024
FUSED_TOKENS = 256
FFN_COLS = 256
CONV_COLS = 256
DELTA_TOKENS = 256
DELTA_ROWS = 2
V7X_VMEM_BYTES = 64 * 1024 * 1024
VMEM_LIMIT = V7X_VMEM_BYTES - 8 * 1024 * 1024


def _resident(shape):
    nd = len(shape)
    return pl.BlockSpec(shape, lambda *_: (0,) * nd, pipeline_mode=pl.Buffered(1))


def _pad_cols(w, used):
    tiles = -(-used // LANE)
    tiles += 1 - tiles % 2
    return jnp.pad(w, ((0, 0), (0, tiles * LANE - w.shape[1])))


def _dot(a, b):
    return jnp.dot(a, b, preferred_element_type=F32)


def _rms(x, w):
    return x * lax.rsqrt(jnp.mean(x * x, axis=-1, keepdims=True) + RMS_EPS) * w


def _silu(x):
    return x * jax.nn.sigmoid(x)


def _gelu(x):
    return 0.5 * x * (1.0 + lax.erf(x * 0.7071067811865476))


def _swiglu_tasks(x_ref, nw_ref, wg_ref, wu_ref, wd_ref, n_scr, a_scr, finish):
    d_ff = wg_ref.shape[1]

    def norm():
        n_scr[...] = _rms(x_ref[...], nw_ref[...]).astype(BF16)

    def chunk(j):
        sl = slice(j * FFN_COLS, (j + 1) * FFN_COLS)
        n = n_scr[...]
        g = _dot(n, wg_ref[:, sl])
        u = _dot(n, wu_ref[:, sl])
        a_scr[:, sl] = (_silu(g) * u).astype(BF16)

    def down():
        finish(x_ref[...] + 0.5 * _dot(a_scr[...], wd_ref[:, 0:x_ref.shape[1]]))

    return [norm] + [functools.partial(chunk, j) for j in range(d_ff // FFN_COLS)] + [down]


def _mix_in_tasks(n_ref, w_ref, cw_ref, carry_scr, qkvg_ref, ba_ref):
    ts, d = n_ref.shape
    assert CONV_K == 4 and cw_ref.shape[0] == CONV_K
    n_heads = d // HEAD
    proj = {}

    def project(part):
        proj[part] = _dot(n_ref[...], w_ref[:, part * d:(part + 1) * d])

    def conv(j):
        sl = slice(j * CONV_COLS, (j + 1) * CONV_COLS)
        off = (j * CONV_COLS) % d
        x = proj[j * CONV_COLS // d][:, off:off + CONV_COLS]
        w0, w1, w2, w3 = (cw_ref[kk:kk + 1, sl] for kk in range(CONV_K))
        xe = jnp.concatenate([carry_scr[:, sl], x], axis=0)
        carry_scr[:, sl] = x[ts - CARRY:, :]
        x2 = pltpu.roll(xe, 2, axis=0)
        acc = w3 * xe + w1 * x2 + pltpu.roll(w2 * xe + w0 * x2, 1, axis=0)
        y = _silu(acc[CARRY:, :])
        for i in range(CONV_COLS // HEAD):
            head = j * (CONV_COLS // HEAD) + i
            yh = y[:, i * HEAD:(i + 1) * HEAD]
            if head < 2 * n_heads:
                yh = yh * lax.rsqrt(jnp.sum(yh * yh, axis=-1, keepdims=True) + L2_EPS)
            qkvg_ref[:, head * HEAD:(head + 1) * HEAD] = yh.astype(BF16)

    def gates():
        n = n_ref[...]
        qkvg_ref[:, 3 * d:4 * d] = _silu(_dot(n, w_ref[:, 3 * d:4 * d])).astype(BF16)
        ba_ref[...] = _dot(n, w_ref[:, 4 * d:4 * d + LANE])

    tasks = []
    for part in range(3):
        tasks.append(functools.partial(project, part))
        tasks += [functools.partial(conv, part * (d // CONV_COLS) + j) for j in range(d // CONV_COLS)]
    return tasks + [gates]


def _interleave(a, b):
    order = sorted([((i + 0.5) / len(a), 0, i) for i in range(len(a))]
                   + [((i + 0.5) / len(b), 1, i) for i in range(len(b))])
    for _, which, i in order:
        (a, b)[which][i]()


def _ffn_mix_kernel(x_ref, nw_ref, wg_ref, wu_ref, wd_ref, ow_ref, wa_ref, cw_ref,
                    h_ref, nn_ref, qkvg_ref, ba_ref,
                    n_scr, a_scr, prev_scr, carry_scr, *, tiles_per_seq):
    i = pl.program_id(0)

    @pl.when(i == 0)
    def _():
        prev_scr[...] = jnp.zeros(prev_scr.shape, BF16)

    @pl.when((i == 0) | (lax.rem(i + tiles_per_seq - 1, tiles_per_seq) == 0))
    def _():
        carry_scr[...] = jnp.zeros(carry_scr.shape, F32)

    slot = lax.rem(i, 2)

    def finish(h):
        normed = _rms(h, ow_ref[...]).astype(BF16)
        h_ref[...] = h
        nn_ref[...] = normed
        prev_scr[slot] = normed

    _interleave(_swiglu_tasks(x_ref, nw_ref, wg_ref, wu_ref, wd_ref, n_scr, a_scr, finish),
                _mix_in_tasks(prev_scr.at[1 - slot], wa_ref, cw_ref, carry_scr, qkvg_ref, ba_ref))


def _ffn_mix(x2d, seq_len, norm_w, wg, wu, wd, out_norm_w, w_a, conv_w):
    t, d = x2d.shape
    tm = FUSED_TOKENS
    assert seq_len % tm == 0 and t % seq_len == 0
    nt = t // tm
    wd = _pad_cols(wd, d)
    cur = lambda w: pl.BlockSpec((tm, w), lambda i: (jnp.minimum(i, nt - 1), 0))
    lag = lambda w: pl.BlockSpec((tm, w), lambda i: (jnp.maximum(i - 1, 0), 0))
    bf = lambda w: jax.ShapeDtypeStruct((t, w), BF16)
    weights = (norm_w, wg, wu, wd, out_norm_w, w_a, conv_w)
    return pl.pallas_call(
        functools.partial(_ffn_mix_kernel, tiles_per_seq=seq_len // tm),
        grid=(nt + 1,),
        in_specs=[cur(d)] + [_resident(w.shape) for w in weights],
        out_specs=[cur(d), cur(d), lag(4 * d), lag(LANE)],
        out_shape=[jax.ShapeDtypeStruct((t, d), F32), bf(d), bf(4 * d),
                   jax.ShapeDtypeStruct((t, LANE), F32)],
        scratch_shapes=[pltpu.VMEM((tm, d), BF16), pltpu.VMEM((tm, wg.shape[1]), BF16),
                        pltpu.VMEM((2, tm, d), BF16), pltpu.VMEM((CARRY, 3 * d), F32)],
        compiler_params=pltpu.CompilerParams(dimension_semantics=("arbitrary",),
                                             vmem_limit_bytes=VMEM_LIMIT),
        name="ffn_mix_in",
    )(x2d, *weights)


def _ffn_final_kernel(x_ref, nw_ref, wg_ref, wu_ref, wd_ref, ow_ref, o_ref, n_scr, a_scr):
    def finish(h):
        o_ref[...] = _rms(h, ow_ref[...])

    for task in _swiglu_tasks(x_ref, nw_ref, wg_ref, wu_ref, wd_ref, n_scr, a_scr, finish):
        task()


def _ffn_final(x2d, norm_w, wg, wu, wd, out_norm_w):
    t, d = x2d.shape
    tm = FFN_TOKENS
    wd = _pad_cols(wd, d)
    row = pl.BlockSpec((tm, d), lambda i: (i, 0))
    weights = (norm_w, wg, wu, wd, out_norm_w)
    return pl.pallas_call(
        _ffn_final_kernel,
        grid=(t // tm,),
        in_specs=[row] + [_resident(w.shape) for w in weights],
        out_specs=row,
        out_shape=jax.ShapeDtypeStruct((t, d), F32),
        scratch_shapes=[pltpu.VMEM((tm, d), BF16), pltpu.VMEM((tm, wg.shape[1]), BF16)],
        compiler_params=pltpu.CompilerParams(dimension_semantics=("arbitrary",),
                                             vmem_limit_bytes=VMEM_LIMIT),
        name="ffn_final",
    )(x2d, *weights)


def _delta_tasks(q_ref, k_ref, v_ref, ba_ref, gs_ref, hp_ref, onw_ref, s_scr, store_out):
    nr, ts, d = q_ref.shape
    n_heads = d // HEAD
    c = CHUNK
    nc = ts // c
    gh = 2 * LANE // c
    n_groups = n_heads // gh
    heads = range(n_heads)
    groups = range(n_groups)
    hsl = [slice(h * HEAD, (h + 1) * HEAD) for h in heads]

    row = lax.broadcasted_iota(jnp.int32, (c, c), 0)
    col = lax.broadcasted_iota(jnp.int32, (c, c), 1)
    tri = (row >= col).astype(F32)
    prow = lax.broadcasted_iota(jnp.int32, (c, gh * c), 0)
    plane = lax.broadcasted_iota(jnp.int32, (c, gh * c), 1)
    pcol = plane % c
    causal = prow >= pcol
    strict = prow > pcol
    diag = prow == pcol
    eye = diag.astype(F32)
    head_of_lane = plane // c
    low_half = lax.broadcasted_iota(jnp.int32, (c, LANE), 1) < c
    scale = HEAD ** -0.5
    zero_blk = jnp.zeros((c, HEAD), BF16)
    zero_state = jnp.zeros((HEAD, HEAD), BF16)
    nt = (((1,), (1,)), ((), ()))

    def block_diag(x):
        return jnp.concatenate([jnp.where(head_of_lane == h, x, jnp.zeros_like(x)) for h in range(gh)],
                               axis=0)

    def lane_blocks(block, h, zero):
        return jnp.concatenate([block if j == h else zero for j in range(gh)], axis=1)

    def pack(cols):
        return jnp.concatenate([jnp.where(low_half, cols[2 * i], cols[2 * i + 1])
                                for i in range(gh // 2)], axis=1)

    def decay_terms(r, rows):
        ba = ba_ref[r, rows, :]
        z = ba + hp_ref[1:2, :]
        softplus = jnp.maximum(z, 0.0) + jnp.log1p(jnp.exp(-jnp.abs(z)))
        la = -jnp.exp(hp_ref[0:1, :]) * softplus
        g_all = jnp.dot(tri, la, preferred_element_type=F32, precision=lax.Precision.HIGHEST)
        g_last = g_all[c - 1:c, :]
        bc = lambda x, lane: jnp.broadcast_to(x[:, lane:lane + 1], (c, LANE))
        eg_all = jnp.exp(g_all)
        ekd_all = jnp.exp(g_last - g_all)
        return dict(beta=[bc(jax.nn.sigmoid(ba), h) for h in heads],
                    g=[bc(g_all, n_heads + h) for h in heads],
                    eg=[bc(eg_all, n_heads + h) for h in heads],
                    ekd=[bc(ekd_all, n_heads + h) for h in heads],
                    egl=jnp.exp(g_last))

    items = [(r, cc) for cc in range(nc) for r in range(nr)]
    rows_of = {cc: slice(cc * c, (cc + 1) * c) for cc in range(nc)}
    streams = [(it, g) for it in items for g in groups]
    t, kh, qkm, ms, ps, sols, states = {}, {}, {}, {}, {}, {}, {}

    def prepare():
        for it in items:
            t[it] = decay_terms(it[0], rows_of[it[1]])
            for h in heads:
                kh[it, h] = k_ref[it[0], rows_of[it[1]], hsl[h]]
        for r in range(nr):
            for h in heads:
                states[r, h] = s_scr[r, h]

    def scores(part):
        for st in part:
            (r, cc), g = st
            hs = range(g * gh, (g + 1) * gh)
            gsl = slice(g * gh * HEAD, (g + 1) * gh * HEAD)
            lhs = jnp.concatenate([q_ref[r, rows_of[cc], gsl], k_ref[r, rows_of[cc], gsl]], axis=0)
            bd_k = jnp.concatenate([lane_blocks(kh[(r, cc), h], h - g * gh, zero_blk) for h in hs], axis=0)
            qk_kk = lax.dot_general(lhs, bd_k, nt, preferred_element_type=F32)
            gcol = pack([t[r, cc]["g"][h] for h in hs])
            grow = jnp.sum(jnp.where(diag, gcol, 0.0), axis=0, keepdims=True)
            decay = jnp.where(causal, jnp.exp(jnp.where(causal, gcol - grow, 0.0)), 0.0)
            qkm[st] = (qk_kk[:c] * (decay * scale)).astype(BF16)
            lmat = jnp.where(strict, pack([t[r, cc]["beta"][h] for h in hs]) * qk_kk[c:] * decay, 0.0)
            ms[st] = (-lmat).astype(BF16)
            ps[st] = eye - lmat
    def square(part):
        for st in part:
            ms[st] = _dot(ms[st], block_diag(ms[st])).astype(BF16)

    def square_and_extend(part):
        for st in part:
            prod = _dot(jnp.concatenate([ms[st], ps[st].astype(BF16)], axis=0), block_diag(ms[st]))
            ms[st] = prod[:c].astype(BF16)
            ps[st] = ps[st] + prod[c:]

    def extend(part):
        for st in part:
            ps[st] = ps[st] + _dot(ps[st].astype(BF16), block_diag(ms[st]))

    def solve(part):
        for st in part:
            (r, cc), g = st
            rhs = []
            for h in range(g * gh, (g + 1) * gh):
                beta, eg = t[r, cc]["beta"][h], t[r, cc]["eg"][h]
                vh = v_ref[r, rows_of[cc], hsl[h]].astype(F32)
                rhs.append(jnp.concatenate([vh * beta, kh[(r, cc), h].astype(F32) * (beta * eg)], axis=1))
            rhs = jnp.concatenate(rhs, axis=0).astype(BF16)
            sols[st] = _dot(block_diag(ps[st].astype(BF16)), rhs)

    def sol_rows(it, h):
        return sols[it, h // gh][(h % gh) * c:(h % gh + 1) * c]

    ws = {}

    def read_state(cc):
        rows = rows_of[cc]
        for r in range(nr):
            for a in range(n_heads // 2):
                pair = (2 * a, 2 * a + 1)
                w_row = jnp.concatenate([sol_rows((r, cc), h)[:, HEAD:] for h in pair], axis=1)
                qd_row = jnp.concatenate(
                    [q_ref[r, rows, hsl[h]].astype(F32) * (t[r, cc]["eg"][h] * scale) for h in pair],
                    axis=1)
                lhs = jnp.concatenate([w_row, qd_row], axis=0).astype(BF16)
                s0, s1 = (states[r, h].astype(BF16) for h in pair)
                rhs = jnp.concatenate([jnp.concatenate([s0, zero_state], axis=1),
                                       jnp.concatenate([zero_state, s1], axis=1)], axis=0)
                ws[cc, r, a] = _dot(lhs, rhs)

    def update_state(cc):
        rows = rows_of[cc]

        def ws_of(r, h):
            return ws[cc, r, h // 2][:, (h % 2) * HEAD:(h % 2 + 1) * HEAD]

        for r in range(nr):
            for g in groups:
                hs = range(g * gh, (g + 1) * gh)
                vn = [(sol_rows((r, cc), h)[:, :HEAD] - ws_of(r, h)[:c]).astype(BF16) for h in hs]
                rhs = jnp.concatenate([lane_blocks(vn[i], i, zero_blk) for i in range(gh)], axis=0)
                kd = [kh[(r, cc), h].astype(F32) * t[r, cc]["ekd"][h] for h in hs]
                kdt = jnp.concatenate([jnp.concatenate([kd[2 * i], kd[2 * i + 1]], axis=0).T
                                       for i in range(gh // 2)], axis=1)
                lhs = jnp.concatenate([qkm[(r, cc), g], kdt.astype(BF16)], axis=0)
                out = _dot(lhs, rhs)
                for i, h in enumerate(hs):
                    osl = slice(i * HEAD, (i + 1) * HEAD)
                    o = ws_of(r, h)[c:] + out[:c, osl]
                    gl = t[r, cc]["egl"][:, n_heads + h:n_heads + h + 1]
                    states[r, h] = states[r, h] * gl + out[c:, osl]
                    on = o * lax.rsqrt(jnp.mean(o * o, axis=-1, keepdims=True) + RMS_EPS) * onw_ref[...]
                    store_out(r, rows, hsl[h], (on * gs_ref[r, rows, hsl[h]].astype(F32)).astype(BF16))

    def save_state():
        for r in range(nr):
            for h in heads:
                s_scr[r, h] = states[r, h]

    quarter = len(streams) // 4
    quarters = [streams[i * quarter:(i + 1) * quarter] for i in range(4)]
    tasks = [prepare] + [functools.partial(scores, p) for p in quarters]
    tasks.append(functools.partial(square, streams))
    sq = c // 2
    while sq > 2:
        tasks.append(functools.partial(square_and_extend, streams))
        sq //= 2
    tasks.append(functools.partial(extend, streams))
    tasks += [functools.partial(solve, p) for p in quarters]
    for cc in range(nc):
        tasks += [functools.partial(read_state, cc), functools.partial(update_state, cc)]
    return tasks + [save_state]


def _mix_out_tasks(h_ref, n_ref, og_ref, wb_ref, lnw_ref, lnb_ref, ws_ref, bs_ref, wo_ref, o_ref,
                   gu_scr, vln_scr, sg_scr, mg_scr):
    nr, ts, d = h_ref.shape
    n_groups = d // HEAD
    gc = GM_CHUNK
    blk = 2 * LANE

    def n_all():
        return jnp.concatenate([n_ref[r] for r in range(nr)], axis=0)

    def gmlp_u(j):
        cols = slice(j * blk, (j + 1) * blk)
        gu_scr[:, cols] = _gelu(_dot(n_all(), wb_ref[:, cols])).astype(BF16)

    def gmlp_v():
        gv = _gelu(_dot(n_all(), wb_ref[:, d:2 * d]))
        mu = jnp.mean(gv, axis=-1, keepdims=True)
        xc = gv - mu
        ln = xc * lax.rsqrt(jnp.mean(xc * xc, axis=-1, keepdims=True) + LN_EPS)
        vln_scr[...] = (ln * lnw_ref[...] + lnb_ref[...]).astype(BF16)

    def gate_unit(h):
        pos_r = lax.broadcasted_iota(jnp.int32, (gc, gc), 0) // CHUNK
        pos_c = lax.broadcasted_iota(jnp.int32, (gc, gc), 1) // CHUNK
        hs = slice(h * HEAD, (h + 1) * HEAD)
        w_m = jnp.where(pos_c <= pos_r, ws_ref[h], 0.0).astype(BF16)
        bias = bs_ref[:, h:h + 1]
        for g in range(nr * ts // gc):
            rows = slice(g * gc, (g + 1) * gc)
            mixed = _dot(w_m, vln_scr[rows, hs]) + bias
            sg_scr[rows, hs] = (gu_scr[rows, hs].astype(F32) * mixed).astype(BF16)

    def merge(j):
        cols = slice(j * blk, (j + 1) * blk)
        n = n_all()
        gate = lambda base: jax.nn.sigmoid(_dot(n, wb_ref[:, base + j * blk:base + (j + 1) * blk]))
        y_a = _dot(og_ref[...], wo_ref[:, j * blk:(j + 1) * blk])
        y_b = _dot(sg_scr[...], wo_ref[:, d + j * blk:d + (j + 1) * blk])
        mg_scr[:, cols] = (gate(2 * d) * y_a + gate(3 * d) * y_b).astype(BF16)

    def project_out(j):
        cols = slice(j * blk, (j + 1) * blk)
        y = _dot(mg_scr[...], wo_ref[:, 2 * d + j * blk:2 * d + (j + 1) * blk])
        for r in range(nr):
            o_ref[r, :, cols] = h_ref[r, :, cols] + y[r * ts:(r + 1) * ts]

    return ([gmlp_v] + [functools.partial(gmlp_u, j) for j in range(d // blk)]
            + [functools.partial(gate_unit, h) for h in range(n_groups)]
            + [functools.partial(merge, j) for j in range(d // blk)]
            + [functools.partial(project_out, j) for j in range(d // blk)])


def _delta_mix_kernel(qkvg_ref, ba_ref, h_ref, n_ref, hp_ref, onw_ref, wb_ref,
                      lnw_ref, lnb_ref, ws_ref, bs_ref, wo_ref, o_ref,
                      s_scr, og_scr, gu_scr, vln_scr, sg_scr, mg_scr, *, tiles_per_seq):
    i = pl.program_id(0)
    nr, ts, d = h_ref.shape
    q_ref, k_ref, v_ref, gs_ref = (qkvg_ref.at[:, :, j * d:(j + 1) * d] for j in range(4))

    @pl.when(i == 0)
    def _():
        og_scr[...] = jnp.zeros(og_scr.shape, BF16)

    @pl.when(lax.rem(i, tiles_per_seq) == 0)
    def _():
        s_scr[...] = jnp.zeros(s_scr.shape, F32)

    slot = lax.rem(i, 2)

    def store_out(r, rows, lanes, value):
        og_scr[slot, r * ts + rows.start:r * ts + rows.stop, lanes] = value

    _interleave(_delta_tasks(q_ref, k_ref, v_ref, ba_ref, gs_ref, hp_ref, onw_ref, s_scr, store_out),
                _mix_out_tasks(h_ref, n_ref, og_scr.at[1 - slot], wb_ref, lnw_ref, lnb_ref, ws_ref,
                               bs_ref, wo_ref, o_ref, gu_scr, vln_scr, sg_scr, mg_scr))


def _delta_mix(qkvg, ba, h, n, hp, onw, wb, ln_w, ln_b, w_s, b_s_t, wo):
    b, s, d = h.shape
    assert HEAD == LANE and 2 * LANE % CHUNK == 0 and (d // HEAD) % (2 * LANE // CHUNK) == 0
    ts, nr = DELTA_TOKENS, DELTA_ROWS
    assert b % nr == 0 and s % ts == 0 and ts % CHUNK == 0 and ts % GM_CHUNK == 0
    tps = s // ts
    nt = (b // nr) * tps

    def at(step):
        return (step // tps, step % tps, 0)

    cur = lambda w: pl.BlockSpec((nr, ts, w), lambda i: at(jnp.minimum(i, nt - 1)))
    lag = lambda w: pl.BlockSpec((nr, ts, w), lambda i: at(jnp.maximum(i - 1, 0)))
    weights = (hp, onw, wb, ln_w, ln_b, w_s, b_s_t, wo)
    rows = nr * ts
    return pl.pallas_call(
        functools.partial(_delta_mix_kernel, tiles_per_seq=tps),
        grid=(nt + 1,),
        in_specs=[cur(4 * d), cur(LANE), lag(d), lag(d)] + [_resident(w.shape) for w in weights],
        out_specs=lag(d),
        out_shape=jax.ShapeDtypeStruct((b, s, d), F32),
        scratch_shapes=[pltpu.VMEM((nr, d // HEAD, HEAD, HEAD), F32), pltpu.VMEM((2, rows, d), BF16)]
                       + [pltpu.VMEM((rows, d), BF16)] * 4,
        compiler_params=pltpu.CompilerParams(dimension_semantics=("arbitrary",),
                                             vmem_limit_bytes=VMEM_LIMIT),
        name="delta_mix_out",
    )(qkvg, ba, h, n, *weights)


def _layer(h, p):
    b, s, d = h.shape
    n_heads = d // HEAD
    row = lambda a: a.reshape(1, -1).astype(F32)
    bf = lambda a: a.astype(BF16)

    w_in = bf(p["w_in"])
    split = 4 * d + 2 * n_heads
    w_a = _pad_cols(w_in[:, :split], 4 * d + LANE)
    w_b = _pad_cols(w_in[:, split:], 4 * d)
    outs = _ffn_mix(h.reshape(b * s, d), s, row(p["ffn1_norm_w"]), bf(p["ffn1_w_gate"]),
                    bf(p["ffn1_w_up"]), bf(p["ffn1_w_down"]), row(p["mix_norm_w"]), w_a,
                    p["dn_conv_w"].astype(F32))
    h1, n, qkvg, ba = (o.reshape(b, s, -1) for o in outs)

    hp = jnp.pad(jnp.stack([p["dn_a_log"], p["dn_dt_bias"]]).astype(F32),
                 ((0, 0), (n_heads, LANE - 2 * n_heads)))
    w_o = _pad_cols(jnp.concatenate([bf(p["dn_w_o"]), bf(p["gmlp_w_o"]), bf(p["w_out"])], axis=1), 3 * d)
    return _delta_mix(qkvg, ba, h1, n, hp, row(p["dn_out_norm_w"]), w_b,
                      row(p["sgu_norm_w"]), row(p["sgu_norm_b"]), p["sgu_w_s"].astype(F32),
                      jnp.swapaxes(p["sgu_b"], 0, 1).astype(F32), w_o)


def kernel(x, ffn1_norm_w, ffn1_w_gate, ffn1_w_up, ffn1_w_down, mix_norm_w, w_in, dn_conv_w, dn_a_log, dn_dt_bias, dn_out_norm_w, dn_w_o, sgu_norm_w, sgu_norm_b, sgu_w_s, sgu_b, gmlp_w_o, w_out, ffn2_norm_w, ffn2_w_gate, ffn2_w_up, ffn2_w_down, final_norm_w):
    assert ffn1_norm_w.shape[0] == 1, "the call pipeline is wired for the fixed depth of one layer"
    b, s, d = x.shape
    l = 0
    p = dict(ffn1_norm_w=ffn1_norm_w[l], ffn1_w_gate=ffn1_w_gate[l], ffn1_w_up=ffn1_w_up[l],
             ffn1_w_down=ffn1_w_down[l], mix_norm_w=mix_norm_w[l], w_in=w_in[l],
             dn_conv_w=dn_conv_w[l], dn_a_log=dn_a_log[l], dn_dt_bias=dn_dt_bias[l],
             dn_out_norm_w=dn_out_norm_w[l], dn_w_o=dn_w_o[l], sgu_norm_w=sgu_norm_w[l],
             sgu_norm_b=sgu_norm_b[l], sgu_w_s=sgu_w_s[l], sgu_b=sgu_b[l], gmlp_w_o=gmlp_w_o[l],
             w_out=w_out[l])
    h2 = _layer(x, p)
    bf = lambda a: a.astype(BF16)
    out = _ffn_final(h2.reshape(b * s, d), ffn2_norm_w[l].reshape(1, d), bf(ffn2_w_gate[l]),
                     bf(ffn2_w_up[l]), bf(ffn2_w_down[l]), final_norm_w.reshape(1, d))
    return out.reshape(b, s, d)
```
